```python
import jax, jax.numpy as jnp
from jax import lax
import numpy as np

D_MODEL = 1024
BATCH = 16
SEQ = 2048
DEPTH = 4

N_HEADS = 16
HEAD_DIM = 64
ROPE_THETA = 10000.0
RMS_EPS = 1e-6
NEG_INF = -1e30
BIG = 1e30

NSA_KV_GROUPS = 2
CMP_BLOCK = 32
CMP_STRIDE = 16
CMP_HIDDEN = 256
SEL_BLOCK = 64
SEL_TOP_N = 16
WINDOW = 512
NSA_Q_BLOCK = 32

DSA_KV_RANK = 256
IDX_HEADS = 8
IDX_DIM = 64
DSA_TOP_K_MAX = 256
DSA_Q_BLOCK = 128

D_FF = 2816
CONV_WIDTH = 3

NSA_Q_W = N_HEADS * HEAD_DIM
NSA_KV_W = NSA_KV_GROUPS * HEAD_DIM
NSA_IN_W = NSA_Q_W + 6 * NSA_KV_W + 3 * N_HEADS
DSA_IN_W = N_HEADS * HEAD_DIM + DSA_KV_RANK + IDX_HEADS * IDX_DIM + IDX_DIM + IDX_HEADS
N_NSA = (DEPTH + 1) // 2
N_DSA = DEPTH // 2

kernel_name = "hybrid_nsa_dsa_convffn_trunk"


def rms_norm(x, g):
    xf = x.astype(jnp.float32)
    y = xf * lax.rsqrt(jnp.mean(xf * xf, axis=-1, keepdims=True) + RMS_EPS)
    return (y * g.astype(jnp.float32)).astype(x.dtype)


def rope(x, pos):
    half = x.shape[-1] // 2
    inv = ROPE_THETA ** (-jnp.arange(half, dtype=jnp.float32) / half)
    ang = pos.astype(jnp.float32)[:, None] * inv[None, :]
    cos, sin = jnp.cos(ang), jnp.sin(ang)
    if x.ndim == 4:
        cos, sin = cos[:, None, :], sin[:, None, :]
    xf = x.astype(jnp.float32)
    x1, x2 = xf[..., :half], xf[..., half:]
    return jnp.concatenate([x1 * cos - x2 * sin, x2 * cos + x1 * sin], axis=-1).astype(x.dtype)


def masked_softmax(s, mask):
    s = jnp.where(mask, s.astype(jnp.float32), NEG_INF)
    return jnp.where(mask, jax.nn.softmax(s, axis=-1), 0.0)


def sweep_query_blocks(fn, seq, q_block):
    starts = jnp.arange(seq // q_block, dtype=jnp.int32) * q_block
    out = lax.map(fn, starts)
    nb, b, q, f = out.shape
    return out.transpose(1, 0, 2, 3).reshape(b, nb * q, f)


def nsa_mixer(h, w_in, pe_k, w1_k, w2_k, pe_v, w1_v, w2_v, w_out):
    B, S, _ = h.shape
    G, HPG, dh = NSA_KV_GROUPS, N_HEADS // NSA_KV_GROUPS, HEAD_DIM
    pos = jnp.arange(S, dtype=jnp.int32)
    proj = h @ w_in
    q, kc, vc, ks, vs, kw, vw, gate = jnp.split(
        proj, np.cumsum([NSA_Q_W] + [NSA_KV_W] * 6).tolist(), axis=-1)
    q = rope(q.reshape(B, S, N_HEADS, dh), pos).reshape(B, S, G, HPG, dh) * dh ** -0.5
    kvshape = (B, S, G, dh)
    ks = rope(ks.reshape(kvshape), pos)
    kw = rope(kw.reshape(kvshape), pos)
    vs = vs.reshape(kvshape)
    vw = vw.reshape(kvshape)
    gate = jax.nn.sigmoid(gate.astype(jnp.float32)).reshape(B, S, G, HPG, 3).astype(h.dtype)

    n_cmp = (S - CMP_BLOCK) // CMP_STRIDE + 1
    cmp_idx = np.arange(n_cmp)[:, None] * CMP_STRIDE + np.arange(CMP_BLOCK)[None, :]
    cmp_end = jnp.asarray(cmp_idx[:, -1], dtype=jnp.int32)

    def compress(xt, pe, w1, w2):
        blocks = xt[:, cmp_idx] + pe[None, None, :, None, :]
        blocks = blocks.transpose(0, 1, 3, 2, 4).reshape(B, n_cmp, G, CMP_BLOCK * dh)
        return jax.nn.gelu(blocks @ w1) @ w2

    k_cmp = rope(compress(kc.reshape(kvshape), pe_k, w1_k, w2_k), cmp_end)
    v_cmp = compress(vc.reshape(kvshape), pe_v, w1_v, w2_v)

    n_sel = S // SEL_BLOCK
    top_n = min(SEL_TOP_N, n_sel)
    ci = np.arange(n_cmp)[:, None]
    sj = np.arange(n_sel)[None, :]
    agg = jnp.asarray((ci * CMP_STRIDE <= sj * SEL_BLOCK + SEL_BLOCK - 1)
                      & (ci * CMP_STRIDE + CMP_BLOCK - 1 >= sj * SEL_BLOCK), dtype=jnp.float32)
    ks_blocks = ks.reshape(B, n_sel, SEL_BLOCK, G, dh).transpose(0, 3, 1, 2, 4)
    vs_blocks = vs.reshape(B, n_sel, SEL_BLOCK, G, dh).transpose(0, 3, 1, 2, 4)
    gather_blocks = jax.vmap(jax.vmap(lambda kb, ib: kb[ib]))
    blk = jnp.arange(n_sel, dtype=jnp.int32)

    kw_pad = jnp.pad(kw, ((0, 0), (WINDOW, 0), (0, 0), (0, 0)))
    vw_pad = jnp.pad(vw, ((0, 0), (WINDOW, 0), (0, 0), (0, 0)))
    Q = NSA_Q_BLOCK

    def block(s0):
        t = s0 + jnp.arange(Q, dtype=jnp.int32)
        qb = lax.dynamic_slice_in_dim(q, s0, Q, axis=1)
        gb = lax.dynamic_slice_in_dim(gate, s0, Q, axis=1)
        s_c = jnp.einsum('bqghd,bngd->bghqn', qb, k_cmp)
        p_c = masked_softmax(s_c, cmp_end[None, :] <= t[:, None])
        o_c = jnp.einsum('bghqn,bngd->bqghd', p_c.astype(v_cmp.dtype), v_cmp)
        imp = jnp.einsum('bghqn,nj->bgqj', p_c, agg)
        cur = t // SEL_BLOCK
        forced = (blk[None, :] == 0) | (blk[None, :] == cur[:, None]) | (blk[None, :] == cur[:, None] - 1)
        causal_blk = blk[None, :] * SEL_BLOCK <= t[:, None]
        imp = jnp.where(causal_blk, jnp.where(forced, BIG, imp), NEG_INF)
        _, sel = lax.top_k(imp, top_n)
        k_s = gather_blocks(ks_blocks, sel).reshape(B, G, Q, top_n * SEL_BLOCK, dh)
        v_s = gather_blocks(vs_blocks, sel).reshape(B, G, Q, top_n * SEL_BLOCK, dh)
        tok = (sel[..., None] * SEL_BLOCK + jnp.arange(SEL_BLOCK, dtype=jnp.int32)).reshape(
            B, G, Q, top_n * SEL_BLOCK)
        m_s = (tok <= t[None, None, :, None])[:, :, None]
        s_s = jnp.einsum('bqghd,bgqkd->bghqk', qb, k_s)
        p_s = masked_softmax(s_s, m_s)
        o_s = jnp.einsum('bghqk,bgqkd->bqghd', p_s.astype(v_s.dtype), v_s)
        k_w = lax.dynamic_slice_in_dim(kw_pad, s0, Q + WINDOW, axis=1)
        v_w = lax.dynamic_slice_in_dim(vw_pad, s0, Q + WINDOW, axis=1)
        kpos = s0 - WINDOW + jnp.arange(Q + WINDOW, dtype=jnp.int32)
        dist = t[:, None] - kpos[None, :]
        m_w = (dist >= 0) & (dist < WINDOW) & (kpos[None, :] >= 0)
        s_w = jnp.einsum('bqghd,bkgd->bghqk', qb, k_w)
        p_w = masked_softmax(s_w, m_w)
        o_w = jnp.einsum('bghqk,bkgd->bqghd', p_w.astype(v_w.dtype), v_w)
        o = gb[..., 0:1] * o_c + gb[..., 1:2] * o_s + gb[..., 2:3] * o_w
        return o.reshape(B, Q, N_HEADS * dh)

    return sweep_query_blocks(block, S, Q) @ w_out


def dsa_mixer(h, w_in, kv_norm, w_uk, w_uv, w_out):
    B, S, _ = h.shape
    dh = HEAD_DIM
    pos = jnp.arange(S, dtype=jnp.int32)
    proj = h @ w_in
    q, c, qi, ki, wi = jnp.split(
        proj, np.cumsum([N_HEADS * dh, DSA_KV_RANK, IDX_HEADS * IDX_DIM, IDX_DIM]).tolist(), axis=-1)
    q = rope(q.reshape(B, S, N_HEADS, dh), pos) * dh ** -0.5
    c = rms_norm(c, kv_norm)
    k = rope(c @ w_uk, pos)
    v = c @ w_uv
    qi = rope(qi.reshape(B, S, IDX_HEADS, IDX_DIM), pos) * IDX_DIM ** -0.5
    ki = rope(ki, pos)
    wi = wi * IDX_HEADS ** -0.5
    top_k = min(DSA_TOP_K_MAX, S // 4)
    gather_keys = jax.vmap(lambda xb, ib: xb[ib])
    Q = DSA_Q_BLOCK

    def block(s0):
        t = s0 + jnp.arange(Q, dtype=jnp.int32)
        qb = lax.dynamic_slice_in_dim(q, s0, Q, axis=1)
        qib = lax.dynamic_slice_in_dim(qi, s0, Q, axis=1)
        wib = lax.dynamic_slice_in_dim(wi, s0, Q, axis=1)
        rel = jax.nn.relu(jnp.einsum('bqhd,bsd->bqhs', qib, ki).astype(jnp.float32))
        score = jnp.einsum('bqh,bqhs->bqs', wib.astype(jnp.float32), rel)
        score = jnp.where(pos[None, :] <= t[:, None], score, NEG_INF)
        _, sel = lax.top_k(score, top_k)
        k_s = gather_keys(k, sel)
        v_s = gather_keys(v, sel)
        s = jnp.einsum('bqhd,bqkd->bhqk', qb, k_s)
        p = masked_softmax(s, (sel <= t[None, :, None])[:, None])
        o = jnp.einsum('bhqk,bqkd->bqhd', p.astype(v_s.dtype), v_s)
        return o.reshape(B, Q, N_HEADS * dh)

    return sweep_query_blocks(block, S, Q) @ w_out


def conv_ffn(h, w_up, conv_w, conv_b, w_down):
    u = h @ w_up
    u = lax.conv_general_dilated(
        u, conv_w[:, None, :], window_strides=(1,), padding=((CONV_WIDTH - 1, 0),),
        dimension_numbers=('NWC', 'WIO', 'NWC'), feature_group_count=u.shape[-1]) + conv_b
    val, gate = jnp.split(u, 2, axis=-1)
    return (jax.nn.gelu(gate) * val) @ w_down


def setup_inputs(seed: int = 0) -> dict:
    key = jax.random.key(seed)
    ks = jax.random.split(key, 20)
    dh = HEAD_DIM

    def nrm(k, shape, scale):
        return jax.random.normal(k, shape, dtype=jnp.float32) * scale

    return {
        "x": nrm(ks[0], (BATCH, SEQ, D_MODEL), 1.0),
        "norm_g": 1.0 + nrm(ks[1], (DEPTH, 4, D_MODEL), 0.02),
        "nsa_w_in": nrm(ks[2], (N_NSA, D_MODEL, NSA_IN_W), D_MODEL ** -0.5),
        "nsa_cmp_pe_k": nrm(ks[3], (N_NSA, CMP_BLOCK, dh), 0.1),
        "nsa_cmp_w1_k": nrm(ks[4], (N_NSA, CMP_BLOCK * dh, CMP_HIDDEN), (CMP_BLOCK * dh) ** -0.5),
        "nsa_cmp_w2_k": nrm(ks[5], (N_NSA, CMP_HIDDEN, dh), CMP_HIDDEN ** -0.5),
        "nsa_cmp_pe_v": nrm(ks[6], (N_NSA, CMP_BLOCK, dh), 0.1),
        "nsa_cmp_w1_v": nrm(ks[7], (N_NSA, CMP_BLOCK * dh, CMP_HIDDEN), (CMP_BLOCK * dh) ** -0.5),
        "nsa_cmp_w2_v": nrm(ks[8], (N_NSA, CMP_HIDDEN, dh), CMP_HIDDEN ** -0.5),
        "nsa_w_out": nrm(ks[9], (N_NSA, N_HEADS * dh, D_MODEL), (N_HEADS * dh) ** -0.5),
        "dsa_w_in": nrm(ks[10], (N_DSA, D_MODEL, DSA_IN_W), D_MODEL ** -0.5),
        "dsa_kv_norm": 1.0 + nrm(ks[11], (N_DSA, DSA_KV_RANK), 0.02),
        "dsa_w_uk": nrm(ks[12], (N_DSA, DSA_KV_RANK, dh), DSA_KV_RANK ** -0.5),
        "dsa_w_uv": nrm(ks[13], (N_DSA, DSA_KV_RANK, dh), DSA_KV_RANK ** -0.5),
        "dsa_w_out": nrm(ks[14], (N_DSA, N_HEADS * dh, D_MODEL), (N_HEADS * dh) ** -0.5),
        "ffn_w_up": nrm(ks[15], (DEPTH, D_MODEL, 2 * D_FF), D_MODEL ** -0.5),
        "ffn_conv_w": nrm(ks[16], (DEPTH, CONV_WIDTH, 2 * D_FF), CONV_WIDTH ** -0.5),
        "ffn_conv_b": nrm(ks[17], (DEPTH, 2 * D_FF), 0.02),
        "ffn_w_down": nrm(ks[18], (DEPTH, D_FF, D_MODEL), D_FF ** -0.5),
    }


def reference(x, norm_g, nsa_w_in, nsa_cmp_pe_k, nsa_cmp_w1_k, nsa_cmp_w2_k,
              nsa_cmp_pe_v, nsa_cmp_w1_v, nsa_cmp_w2_v, nsa_w_out,
              dsa_w_in, dsa_kv_norm, dsa_w_uk, dsa_w_uv, dsa_w_out,
              ffn_w_up, ffn_conv_w, ffn_conv_b, ffn_w_down):
    for i in range(DEPTH):
        h = rms_norm(x, norm_g[i, 0])
        if i % 2 == 0:
            j = i // 2
            y = nsa_mixer(h, nsa_w_in[j], nsa_cmp_pe_k[j], nsa_cmp_w1_k[j], nsa_cmp_w2_k[j],
                          nsa_cmp_pe_v[j], nsa_cmp_w1_v[j], nsa_cmp_w2_v[j], nsa_w_out[j])
        else:
            j = i // 2
            y = dsa_mixer(h, dsa_w_in[j], dsa_kv_norm[j], dsa_w_uk[j], dsa_w_uv[j], dsa_w_out[j])
        x = x + rms_norm(y, norm_g[i, 1])
        y = conv_ffn(rms_norm(x, norm_g[i, 2]), ffn_w_up[i], ffn_conv_w[i], ffn_conv_b[i], ffn_w_down[i])
        x = x + rms_norm(y, norm_g[i, 3])
    return x
```

```python
import functools

import numpy as np
import jax
import jax.numpy as jnp
from jax import lax
from jax.experimental import pallas as pl
from jax.experimental.pallas import tpu as pltpu

F32 = jnp.float32
BF16 = jnp.bfloat16

D_MODEL = 1024
DEPTH = 4
N_HEADS = 16
HEAD_DIM = 64
ROPE_THETA = 10000.0
RMS_EPS = 1e-6
NEG_INF = -1e30
BIG = 1e30

NSA_KV_GROUPS = 2
CMP_BLOCK = 32
CMP_STRIDE = 16
CMP_HIDDEN = 256
SEL_BLOCK = 64
SEL_TOP_N = 16
WINDOW = 512

DSA_KV_RANK = 256
IDX_HEADS = 8
IDX_DIM = 64
DSA_TOP_K_MAX = 256

D_FF = 2816
CONV_WIDTH = 3

NSA_IN_W = N_HEADS * HEAD_DIM + 6 * NSA_KV_GROUPS * HEAD_DIM + 3 * N_HEADS
DSA_IN_W = N_HEADS * HEAD_DIM + DSA_KV_RANK + IDX_HEADS * IDX_DIM + IDX_DIM + IDX_HEADS

LANES = 128
IN_W_PAD = 1920
TQ = 128
CK = 256
VMEM_LIMIT = 56 * 1024 * 1024


def _cparams(sem):
    return pltpu.CompilerParams(dimension_semantics=sem, vmem_limit_bytes=VMEM_LIMIT)


def _rms(x, g):
    return x * lax.rsqrt(jnp.mean(x * x, axis=-1, keepdims=True) + RMS_EPS) * g


def _lane_iota(shape):
    return lax.broadcasted_iota(jnp.int32, shape, len(shape) - 1)


def _rope128(x, cos2, sin2):
    lane = _lane_iota(x.shape)
    first = (lane % 64) < 32
    swapped = jnp.where(first, pltpu.roll(x, 96, axis=1), pltpu.roll(x, 32, axis=1))
    return x * cos2 + swapped * sin2


def _dup_halves(x):
    lo = _lane_iota(x.shape) < 64
    r = pltpu.roll(x, 64, axis=1)
    return jnp.where(lo, x, r), jnp.where(lo, r, x)


def _dot(a, b):
    return jnp.dot(a, b, preferred_element_type=F32)


def _dot_nt(a, b):
    return lax.dot_general(a, b, (((1,), (1,)), ((), ())), preferred_element_type=F32)


def _gelu_tanh(x):
    return 0.5 * x * (1.0 + jnp.tanh(0.7978845608028654 * (x + 0.044715 * (x * x * x))))


def _flash_update(s, mask, v, m_ref, l_ref, acc_ref):
    rows, n = s.shape
    r = rows // TQ
    mask3 = mask[None]
    s3 = jnp.where(mask3, s.reshape(r, TQ, n), NEG_INF)
    m_prev = m_ref[...]
    m_cur = jnp.max(s3, axis=-1, keepdims=True).reshape(rows, 1)
    m_new = jnp.maximum(m_prev, m_cur)
    alpha = jnp.exp(m_prev - m_new)
    p3 = jnp.where(mask3, jnp.exp(s3 - m_new[:, :1].reshape(r, TQ, 1)), 0.0)
    l_ref[...] = alpha * l_ref[...] + jnp.sum(p3, axis=-1, keepdims=True).reshape(rows, 1)
    acc_ref[...] = alpha * acc_ref[...] + _dot(p3.reshape(rows, n).astype(BF16), v)
    m_ref[...] = m_new


def _flash_init(m_ref, l_ref, acc_ref):
    m_ref[...] = jnp.full(m_ref.shape, NEG_INF, F32)
    l_ref[...] = jnp.zeros(l_ref.shape, F32)
    acc_ref[...] = jnp.zeros(acc_ref.shape, F32)


def _flash_out(l_ref, acc_ref):
    l = l_ref[...]
    return jnp.where(l > 0.0, acc_ref[...] / jnp.where(l > 0.0, l, 1.0), 0.0)


def _nsa_in_kernel(x_ref, g_ref, w_ref, cos_ref, sin_ref, q_ref, kv_ref, cv_ref, gate_ref):
    h = _rms(x_ref[...], g_ref[...]).astype(BF16)
    proj = _dot(h, w_ref[...])
    cos2 = cos_ref[...]
    sin2 = sin_ref[...]
    for m in range(8):
        slab = proj[:, m * LANES:(m + 1) * LANES]
        q_ref[:, m * LANES:(m + 1) * LANES] = (_rope128(slab, cos2, sin2) * (HEAD_DIM ** -0.5)).astype(BF16)
    cv_ref[...] = proj[:, 1024:1280]
    ks = _rope128(proj[:, 1280:1408], cos2, sin2)
    vs = proj[:, 1408:1536]
    kw = _rope128(proj[:, 1536:1664], cos2, sin2)
    vw = proj[:, 1664:1792]
    for j, t in enumerate((ks, vs, kw, vw)):
        d0, d1 = _dup_halves(t)
        kv_ref[:, j * LANES:(j + 1) * LANES] = d0.astype(BF16)
        kv_ref[:, (4 + j) * LANES:(5 + j) * LANES] = d1.astype(BF16)
    gt = proj[:, 1792:1920]
    sg = 1.0 / (1.0 + jnp.exp(-gt))
    gate_ref[:, 0:LANES] = sg
    gate_ref[:, LANES:2 * LANES] = pltpu.roll(sg, LANES - 24, axis=1)


def _nsa_in_proj(x2d, g, w, cos2, sin2, S):
    T = x2d.shape[0]
    tm = 256
    tps = S // tm
    return pl.pallas_call(
        _nsa_in_kernel,
        grid=(T // tm,),
        in_specs=[
            pl.BlockSpec((tm, D_MODEL), lambda i: (i, 0)),
            pl.BlockSpec((1, D_MODEL), lambda i: (0, 0)),
            pl.BlockSpec((D_MODEL, IN_W_PAD), lambda i: (0, 0)),
            pl.BlockSpec((tm, LANES), lambda i: (i % tps, 0)),
            pl.BlockSpec((tm, LANES), lambda i: (i % tps, 0)),
        ],
        out_specs=[
            pl.BlockSpec((tm, 1024), lambda i: (i, 0)),
            pl.BlockSpec((tm, 1024), lambda i: (i, 0)),
            pl.BlockSpec((tm, 256), lambda i: (i, 0)),
            pl.BlockSpec((tm, 256), lambda i: (i, 0)),
        ],
        out_shape=[
            jax.ShapeDtypeStruct((T, 1024), BF16),
            jax.ShapeDtypeStruct((T, 1024), BF16),
            jax.ShapeDtypeStruct((T, 256), F32),
            jax.ShapeDtypeStruct((T, 256), F32),
        ],
        compiler_params=_cparams(("parallel",)),
        name="nsa_in_proj",
    )(x2d, g, w, cos2, sin2)


def _nsa_cmp_kernel(blk_ref, pe_ref, w1_ref, w2_ref, cos_ref, sin_ref, o_ref):
    xb = (blk_ref[0, 0] + pe_ref[0]).astype(BF16)
    hid = _gelu_tanh(_dot(xb, w1_ref[0]))
    out = _dot(hid.astype(BF16), w2_ref[0])
    o_ref[0] = _rope128(out, cos_ref[0], sin_ref[0]).astype(BF16)


def _nsa_compress(blocks, pe, w1, w2d, cosc, sinc):
    B, _, ncp, width = blocks.shape
    return pl.pallas_call(
        _nsa_cmp_kernel,
        grid=(B, 4),
        in_specs=[
            pl.BlockSpec((1, 1, ncp, width), lambda b, j: (b, j, 0, 0)),
            pl.BlockSpec((1, 1, width), lambda b, j: (j // 2, 0, 0)),
            pl.BlockSpec((1, width, CMP_HIDDEN), lambda b, j: (j // 2, 0, 0)),
            pl.BlockSpec((1, CMP_HIDDEN, LANES), lambda b, j: (j // 2, 0, 0)),
            pl.BlockSpec((1, ncp, LANES), lambda b, j: (j // 2, 0, 0)),
            pl.BlockSpec((1, ncp, LANES), lambda b, j: (j // 2, 0, 0)),
        ],
        out_specs=pl.BlockSpec((1, ncp, LANES), lambda b, j: (b, 0, j)),
        out_shape=jax.ShapeDtypeStruct((B, ncp, 4 * LANES), BF16),
        compiler_params=_cparams(("parallel", "parallel")),
        name="nsa_compress",
    )(blocks, pe, w1, w2d, cosc, sinc)


def _nsa_attn_kernel(q_ref, kv_ref, cmp_ref, gate_ref, agg_ref, e_ref, o_ref,
                     selk_ref, ma, la, acca, mb, lb, accb, *, S):
    n_sel = S // SEL_BLOCK
    top_n = min(SEL_TOP_N, n_sel)
    i = pl.program_id(2)
    s0 = i * TQ
    lo = _lane_iota((1, LANES)) < 64
    q = q_ref[0]
    q2 = jnp.concatenate([q[:, p * LANES:(p + 1) * LANES] for p in range(4)], axis=0)
    zero = jnp.zeros_like(q2)
    q_lo = jnp.where(lo, q2, zero)
    q_hi = jnp.where(lo, zero, q2)
    t = s0 + lax.broadcasted_iota(jnp.int32, (TQ, 1), 0)

    kc = cmp_ref[0, :, 0:LANES]
    vc = cmp_ref[0, :, LANES:2 * LANES]
    ncp = kc.shape[0]
    cend = _lane_iota((1, ncp)) * CMP_STRIDE + (CMP_BLOCK - 1)
    mask_c = (cend <= t)[None]

    def cmp_probs(qh):
        s3 = jnp.where(mask_c, _dot_nt(qh, kc).reshape(4, TQ, ncp), NEG_INF)
        e = jnp.where(mask_c, jnp.exp(s3 - jnp.max(s3, axis=-1, keepdims=True)), 0.0)
        l = jnp.sum(e, axis=-1, keepdims=True)
        return e / jnp.where(l > 0.0, l, 1.0)

    pa3 = cmp_probs(q_lo)
    pb3 = cmp_probs(q_hi)
    oc_a = _dot(pa3.reshape(4 * TQ, ncp).astype(BF16), vc)
    oc_b = _dot(pb3.reshape(4 * TQ, ncp).astype(BF16), vc)
    psum = jnp.sum(pa3, axis=0) + jnp.sum(pb3, axis=0)
    agg = agg_ref[...]
    p_hi = psum.astype(BF16)
    r1 = psum - p_hi.astype(F32)
    p_mid = r1.astype(BF16)
    p_lo = (r1 - p_mid.astype(F32)).astype(BF16)
    imp = _dot(p_hi, agg) + _dot(p_mid, agg) + _dot(p_lo, agg)

    jblk = _lane_iota((1, LANES))
    cur = jnp.right_shift(t, SEL_BLOCK.bit_length() - 1)
    forced = (jblk == 0) | (jblk == cur) | (jblk == cur - 1)
    causal_blk = jblk * SEL_BLOCK <= t
    impm = jnp.where(causal_blk, jnp.where(forced, BIG, imp), NEG_INF)
    rank = jnp.zeros((TQ, LANES), F32)
    for k in range(n_sel):
        col = impm[:, k:k + 1]
        beats = (col > impm) | ((col == impm) & (jblk > k))
        rank = rank + jnp.where(beats, 1.0, 0.0)
    sel = jnp.where(rank < float(top_n), 1.0, 0.0).astype(BF16)
    selk_ref[...] = _dot(sel, e_ref[...])

    _flash_init(ma, la, acca)
    _flash_init(mb, lb, accb)

    def sel_body(c, carry):
        off = pl.multiple_of(c * CK, CK)
        k = kv_ref[0, pl.ds(off, CK), 0:LANES]
        v = kv_ref[0, pl.ds(off, CK), LANES:2 * LANES]
        kpos = off + _lane_iota((1, CK))
        mask = (selk_ref[:, pl.ds(off, CK)] > 0.5) & (kpos <= t)
        _flash_update(_dot_nt(q_lo, k), mask, v, ma, la, acca)
        _flash_update(_dot_nt(q_hi, k), mask, v, mb, lb, accb)
        return carry

    lax.fori_loop(0, s0 // CK + 1, sel_body, 0)
    os_a = _flash_out(la, acca)
    os_b = _flash_out(lb, accb)

    _flash_init(ma, la, acca)
    _flash_init(mb, lb, accb)

    def win_body(c, carry):
        off = pl.multiple_of(c * CK, CK)
        k = kv_ref[0, pl.ds(off, CK), 2 * LANES:3 * LANES]
        v = kv_ref[0, pl.ds(off, CK), 3 * LANES:4 * LANES]
        dist = t - (off + _lane_iota((1, CK)))
        mask = (dist >= 0) & (dist < WINDOW)
        _flash_update(_dot_nt(q_lo, k), mask, v, ma, la, acca)
        _flash_update(_dot_nt(q_hi, k), mask, v, mb, lb, accb)
        return carry

    c_lo = jnp.maximum(s0 - (WINDOW - 1), 0) // CK
    lax.fori_loop(c_lo, s0 // CK + 1, win_body, 0)
    ow_a = _flash_out(la, acca)
    ow_b = _flash_out(lb, accb)

    gt = gate_ref[0]
    for p in range(4):
        rs = slice(p * TQ, (p + 1) * TQ)
        ca, cb = (2 * p) * 3, (2 * p + 1) * 3
        out_a = gt[:, ca:ca + 1] * oc_a[rs] + gt[:, ca + 1:ca + 2] * os_a[rs] + gt[:, ca + 2:ca + 3] * ow_a[rs]
        out_b = gt[:, cb:cb + 1] * oc_b[rs] + gt[:, cb + 1:cb + 2] * os_b[rs] + gt[:, cb + 2:cb + 3] * ow_b[rs]
        o_ref[0, :, p * LANES:(p + 1) * LANES] = jnp.where(lo, out_a, out_b).astype(BF16)


def _nsa_attention(q, kv, cmpkv, gate, agg, emat, S):
    B = q.shape[0]
    ncp = cmpkv.shape[1]
    stat = pltpu.VMEM((4 * TQ, LANES), F32)
    return pl.pallas_call(
        functools.partial(_nsa_attn_kernel, S=S),
        grid=(B, NSA_KV_GROUPS, S // TQ),
        in_specs=[
            pl.BlockSpec((1, TQ, 512), lambda b, g, i: (b, i, g)),
            pl.BlockSpec((1, S, 512), lambda b, g, i: (b, 0, g)),
            pl.BlockSpec((1, ncp, 2 * LANES), lambda b, g, i: (b, 0, g)),
            pl.BlockSpec((1, TQ, LANES), lambda b, g, i: (b, i, g)),
            pl.BlockSpec((ncp, LANES), lambda b, g, i: (0, 0)),
            pl.BlockSpec((LANES, S), lambda b, g, i: (0, 0)),
        ],
        out_specs=pl.BlockSpec((1, TQ, 512), lambda b, g, i: (b, i, g)),
        out_shape=jax.ShapeDtypeStruct((B, S, 1024), BF16),
        scratch_shapes=[pltpu.VMEM((TQ, S), F32), stat, stat, stat, stat, stat, stat],
        compiler_params=_cparams(("parallel", "parallel", "parallel")),
        name="nsa_attention",
    )(q, kv, cmpkv, gate, agg, emat)


def _dsa_in_kernel(x_ref, g_ref, w_ref, kvn_ref, wukv_ref, cos_ref, sin_ref,
                   q_ref, kv_ref, qi_ref, ki_ref, wi_ref):
    h = _rms(x_ref[...], g_ref[...]).astype(BF16)
    proj = _dot(h, w_ref[...])
    cos2 = cos_ref[...]
    sin2 = sin_ref[...]
    for m in range(8):
        slab = proj[:, m * LANES:(m + 1) * LANES]
        q_ref[:, m * LANES:(m + 1) * LANES] = (_rope128(slab, cos2, sin2) * (HEAD_DIM ** -0.5)).astype(BF16)
    c = _rms(proj[:, 1024:1280], kvn_ref[...]).astype(BF16)
    kvu = _dot(c, wukv_ref[...])
    kd, vd = _dup_halves(kvu)
    kv_ref[:, 0:LANES] = _rope128(kd, cos2, sin2).astype(BF16)
    kv_ref[:, LANES:2 * LANES] = vd.astype(BF16)
    for m in range(4):
        slab = proj[:, 1280 + m * LANES:1280 + (m + 1) * LANES]
        qi_ref[:, m * LANES:(m + 1) * LANES] = (_rope128(slab, cos2, sin2) * (IDX_DIM ** -0.5)).astype(BF16)
    last = proj[:, 1792:1920]
    kid, wid = _dup_halves(last)
    ki_ref[...] = _rope128(kid, cos2, sin2).astype(BF16)
    wi_ref[...] = wid * (IDX_HEADS ** -0.5)


def _dsa_in_proj(x2d, g, w, kvn, wukv, cos2, sin2, S):
    T = x2d.shape[0]
    tm = 256
    tps = S // tm
    return pl.pallas_call(
        _dsa_in_kernel,
        grid=(T // tm,),
        in_specs=[
            pl.BlockSpec((tm, D_MODEL), lambda i: (i, 0)),
            pl.BlockSpec((1, D_MODEL), lambda i: (0, 0)),
            pl.BlockSpec((D_MODEL, IN_W_PAD), lambda i: (0, 0)),
            pl.BlockSpec((1, DSA_KV_RANK), lambda i: (0, 0)),
            pl.BlockSpec((DSA_KV_RANK, LANES), lambda i: (0, 0)),
            pl.BlockSpec((tm, LANES), lambda i: (i % tps, 0)),
            pl.BlockSpec((tm, LANES), lambda i: (i % tps, 0)),
        ],
        out_specs=[
            pl.BlockSpec((tm, 1024), lambda i: (i, 0)),
            pl.BlockSpec((tm, 256), lambda i: (i, 0)),
            pl.BlockSpec((tm, 512), lambda i: (i, 0)),
            pl.BlockSpec((tm, LANES), lambda i: (i, 0)),
            pl.BlockSpec((tm, LANES), lambda i: (i, 0)),
        ],
        out_shape=[
            jax.ShapeDtypeStruct((T, 1024), BF16),
            jax.ShapeDtypeStruct((T, 256), BF16),
            jax.ShapeDtypeStruct((T, 512), BF16),
            jax.ShapeDtypeStruct((T, LANES), BF16),
            jax.ShapeDtypeStruct((T, LANES), F32),
        ],
        compiler_params=_cparams(("parallel",)),
        name="dsa_in_proj",
    )(x2d, g, w, kvn, wukv, cos2, sin2)


def _sortable(v):
    bits = lax.bitcast_convert_type(v + 0.0, jnp.int32)
    return bits ^ ((bits >> 31) & jnp.int32(0x7FFFFFFF))


_NEG_BITS = np.array([NEG_INF], np.float32).view(np.int32)
_NEG_KEY = int((_NEG_BITS ^ ((_NEG_BITS >> 31) & np.int32(0x7FFFFFFF)))[0])
_INT_MIN = -(2 ** 31)


def _dsa_attn_kernel(q_ref, kv_ref, qi_ref, ki_ref, wi_ref, o_ref,
                     key_ref, selm_ref, ma, la, acca, mb, lb, accb, *, S):
    top_k = min(DSA_TOP_K_MAX, S // 4)
    idx_bits = (S - 1).bit_length()
    i = pl.program_id(1)
    s0 = i * TQ
    n_chunks = s0 // CK + 1
    lo = _lane_iota((1, LANES)) < 64
    t = s0 + lax.broadcasted_iota(jnp.int32, (TQ, 1), 0)

    qi = qi_ref[0]
    qi2 = jnp.concatenate([qi[:, p * LANES:(p + 1) * LANES] for p in range(4)], axis=0)
    zi = jnp.zeros_like(qi2)
    qi_lo = jnp.where(lo, qi2, zi)
    qi_hi = jnp.where(lo, zi, qi2)
    wi = wi_ref[0]
    w_a = jnp.stack([wi[:, 2 * p:2 * p + 1] for p in range(4)], axis=0)
    w_b = jnp.stack([wi[:, 2 * p + 1:2 * p + 2] for p in range(4)], axis=0)

    def idx_body(c, carry):
        off = pl.multiple_of(c * CK, CK)
        ki = ki_ref[0, pl.ds(off, CK), :]
        ra = jnp.maximum(_dot_nt(qi_lo, ki), 0.0).reshape(4, TQ, CK)
        rb = jnp.maximum(_dot_nt(qi_hi, ki), 0.0).reshape(4, TQ, CK)
        score = jnp.sum(ra * w_a, axis=0) + jnp.sum(rb * w_b, axis=0)
        kpos = off + _lane_iota((1, CK))
        score = jnp.where(kpos <= t, score, NEG_INF)
        key_ref[:, pl.ds(off, CK)] = _sortable(score)
        return carry

    lax.fori_loop(0, n_chunks, idx_body, 0)

    n_out = (S - n_chunks * CK).astype(F32)

    def count(pred_fn):
        def body(c, acc):
            off = pl.multiple_of(c * CK, CK)
            return acc + jnp.where(pred_fn(key_ref[:, pl.ds(off, CK)], off), 1.0, 0.0)
        acc = lax.fori_loop(0, n_chunks, body, jnp.zeros((TQ, CK), F32))
        return jnp.sum(acc, axis=-1, keepdims=True)

    def bit_body(it, tu):
        bit = jnp.left_shift(jnp.int32(1), 31 - it)
        cand = (tu | bit) ^ jnp.int32(_INT_MIN)
        cnt = count(lambda kk, off: kk >= cand)
        cnt = cnt + n_out * jnp.where(jnp.int32(_NEG_KEY) >= cand, 1.0, 0.0)
        return jnp.where(cnt >= float(top_k), tu | bit, tu)

    tu = lax.fori_loop(0, 32, bit_body, jnp.zeros((TQ, 1), jnp.int32))
    thr = tu ^ jnp.int32(_INT_MIN)
    n_gt = count(lambda kk, off: kk > thr)
    n_gt = n_gt + n_out * jnp.where(jnp.int32(_NEG_KEY) > thr, 1.0, 0.0)
    need = float(top_k) - n_gt

    def tie_body(it, jb):
        bit = jnp.left_shift(jnp.int32(1), idx_bits - 1 - it)
        cand = jb | bit
        cnt = count(lambda kk, off: (kk == thr) & ((off + _lane_iota((1, CK))) < cand))
        return jnp.where(cnt < need, cand, jb)

    jb = lax.fori_loop(0, idx_bits, tie_body, jnp.zeros((TQ, 1), jnp.int32))

    def selm_body(c, carry):
        off = pl.multiple_of(c * CK, CK)
        kk = key_ref[:, pl.ds(off, CK)]
        kpos = off + _lane_iota((1, CK))
        picked = ((kk > thr) | ((kk == thr) & (kpos <= jb))) & (kpos <= t)
        selm_ref[:, pl.ds(off, CK)] = jnp.where(picked, 1.0, 0.0)
        return carry

    lax.fori_loop(0, n_chunks, selm_body, 0)

    q = q_ref[0]
    q2 = jnp.concatenate([q[:, p * LANES:(p + 1) * LANES] for p in range(8)], axis=0)
    zq = jnp.zeros_like(q2)
    q_lo = jnp.where(lo, q2, zq)
    q_hi = jnp.where(lo, zq, q2)
    _flash_init(ma, la, acca)
    _flash_init(mb, lb, accb)

    def att_body(c, carry):
        off = pl.multiple_of(c * CK, CK)
        k = kv_ref[0, pl.ds(off, CK), 0:LANES]
        v = kv_ref[0, pl.ds(off, CK), LANES:2 * LANES]
        mask = selm_ref[:, pl.ds(off, CK)] > 0.5
        _flash_update(_dot_nt(q_lo, k), mask, v, ma, la, acca)
        _flash_update(_dot_nt(q_hi, k), mask, v, mb, lb, accb)
        return carry

    lax.fori_loop(0, n_chunks, att_body, 0)
    o_a = _flash_out(la, acca)
    o_b = _flash_out(lb, accb)
    for p in range(8):
        rs = slice(p * TQ, (p + 1) * TQ)
        o_ref[0, :, p * LANES:(p + 1) * LANES] = jnp.where(lo, o_a[rs], o_b[rs]).astype(BF16)


def _dsa_attention(q, kv, qi, ki, wi, S):
    B = q.shape[0]
    stat = pltpu.VMEM((8 * TQ, LANES), F32)
    return pl.pallas_call(
        functools.partial(_dsa_attn_kernel, S=S),
        grid=(B, S // TQ),
        in_specs=[
            pl.BlockSpec((1, TQ, 1024), lambda b, i: (b, i, 0)),
            pl.BlockSpec((1, S, 256), lambda b, i: (b, 0, 0)),
            pl.BlockSpec((1, TQ, 512), lambda b, i: (b, i, 0)),
            pl.BlockSpec((1, S, LANES), lambda b, i: (b, 0, 0)),
            pl.BlockSpec((1, TQ, LANES), lambda b, i: (b, i, 0)),
        ],
        out_specs=pl.BlockSpec((1, TQ, 1024), lambda b, i: (b, i, 0)),
        out_shape=jax.ShapeDtypeStruct((B, S, 1024), BF16),
        scratch_shapes=[pltpu.VMEM((TQ, S), jnp.int32), pltpu.VMEM((TQ, S), F32),
                        stat, stat, stat, stat, stat, stat],
        compiler_params=_cparams(("parallel", "parallel")),
        name="dsa_attention",
    )(q, kv, qi, ki, wi)


def _out_proj_kernel(a_ref, w_ref, x_ref, g_ref, o_ref):
    y = _dot(a_ref[...], w_ref[...])
    o_ref[...] = x_ref[...] + _rms(y, g_ref[...])


def _out_proj(a2d, w, x2d, g):
    T = x2d.shape[0]
    tm = 512
    return pl.pallas_call(
        _out_proj_kernel,
        grid=(T // tm,),
        in_specs=[
            pl.BlockSpec((tm, 1024), lambda i: (i, 0)),
            pl.BlockSpec((1024, D_MODEL), lambda i: (0, 0)),
            pl.BlockSpec((tm, D_MODEL), lambda i: (i, 0)),
            pl.BlockSpec((1, D_MODEL), lambda i: (0, 0)),
        ],
        out_specs=pl.BlockSpec((tm, D_MODEL), lambda i: (i, 0)),
        out_shape=jax.ShapeDtypeStruct((T, D_MODEL), F32),
        compiler_params=_cparams(("parallel",)),
        name="mixer_out_proj",
    )(a2d, w, x2d, g)


FFN_TM = 512
FFN_FC = 256
FFN_NF = D_FF // FFN_FC
FFN_HALO = 16


def _ffn_kernel(x_ref, xh_ref, g2_ref, g3_ref, wv_ref, wg_ref, cwv_ref, cwg_ref, cbv_ref, cbg_ref,
                wd_ref, o_ref, hext, uv, ug, acc, *, tiles_per_seq):
    i = pl.program_id(0)
    f = pl.program_id(1)
    tm = FFN_TM

    @pl.when(f == 0)
    def _():
        g2 = g2_ref[...]
        hext[FFN_HALO:, :] = _rms(x_ref[...], g2).astype(BF16)
        hh = _rms(xh_ref[...], g2)
        hh = jnp.where(i % tiles_per_seq == 0, 0.0, hh)
        hext[0:FFN_HALO, :] = hh.astype(BF16)
        acc[...] = jnp.zeros(acc.shape, F32)

    he = hext[...]
    uv[...] = _dot(he, wv_ref[...])
    ug[...] = _dot(he, wg_ref[...])

    def conv(u_ref, cw_ref, cb_ref):
        w = cw_ref[...]
        return (w[2:3] * u_ref[FFN_HALO:FFN_HALO + tm, :]
                + w[1:2] * u_ref[FFN_HALO - 1:FFN_HALO - 1 + tm, :]
                + w[0:1] * u_ref[FFN_HALO - 2:FFN_HALO - 2 + tm, :]
                + cb_ref[...])

    val = conv(uv, cwv_ref, cbv_ref)
    gate = conv(ug, cwg_ref, cbg_ref)
    act = (_gelu_tanh(gate) * val).astype(BF16)
    acc[...] += _dot(act, wd_ref[...])

    @pl.when(f == FFN_NF - 1)
    def _():
        o_ref[...] = x_ref[...] + _rms(acc[...], g3_ref[...])


def _ffn(x2d, g2, g3, wup, cw, cb, wd, S):
    T = x2d.shape[0]
    tm = FFN_TM
    hb = tm // FFN_HALO
    return pl.pallas_call(
        functools.partial(_ffn_kernel, tiles_per_seq=S // tm),
        grid=(T // tm, FFN_NF),
        in_specs=[
            pl.BlockSpec((tm, D_MODEL), lambda i, f: (i, 0)),
            pl.BlockSpec((FFN_HALO, D_MODEL), lambda i, f: (jnp.maximum(i * hb - 1, 0), 0)),
            pl.BlockSpec((1, D_MODEL), lambda i, f: (0, 0)),
            pl.BlockSpec((1, D_MODEL), lambda i, f: (0, 0)),
            pl.BlockSpec((D_MODEL, FFN_FC), lambda i, f: (0, f)),
            pl.BlockSpec((D_MODEL, FFN_FC), lambda i, f: (0, FFN_NF + f)),
            pl.BlockSpec((CONV_WIDTH, FFN_FC), lambda i, f: (0, f)),
            pl.BlockSpec((CONV_WIDTH, FFN_FC), lambda i, f: (0, FFN_NF + f)),
            pl.BlockSpec((1, FFN_FC), lambda i, f: (0, f)),
            pl.BlockSpec((1, FFN_FC), lambda i, f: (0, FFN_NF + f)),
            pl.BlockSpec((FFN_FC, D_MODEL), lambda i, f: (f, 0)),
        ],
        out_specs=pl.BlockSpec((tm, D_MODEL), lambda i, f: (i, 0)),
        out_shape=jax.ShapeDtypeStruct((T, D_MODEL), F32),
        scratch_shapes=[
            pltpu.VMEM((tm + FFN_HALO, D_MODEL), BF16),
            pltpu.VMEM((tm + FFN_HALO, FFN_FC), F32),
            pltpu.VMEM((tm + FFN_HALO, FFN_FC), F32),
            pltpu.VMEM((tm, D_MODEL), F32),
        ],
        compiler_params=_cparams(("parallel", "arbitrary")),
        name="conv_ffn",
    )(x2d, x2d, g2, g3, wup, wup, cw, cw, cb, cb, wd)


def _rope_tables(pos):
    half = HEAD_DIM // 2
    inv = ROPE_THETA ** (-jnp.arange(half, dtype=F32) / half)
    ang = pos.astype(F32)[:, None] * inv[None, :]
    cos, sin = jnp.cos(ang), jnp.sin(ang)
    return jnp.tile(cos, (1, 4)), jnp.tile(jnp.concatenate([-sin, sin], axis=-1), (1, 2))


def _pad_cols(w, width):
    return jnp.pad(w, ((0, 0), (0, width - w.shape[1])))


def _nsa_mixer(x2d, g_pre, g_post, w_in, pe_k, w1_k, w2_k, pe_v, w1_v, w2_v, w_out, B, S, tabs):
    cos2, sin2, cosc, sinc, agg, emat = tabs
    ncp = S // CMP_STRIDE
    q, kv, cv, gate = _nsa_in_proj(x2d, g_pre, _pad_cols(w_in, IN_W_PAD).astype(BF16), cos2, sin2, S)
    chunks = cv.reshape(B, ncp, CMP_STRIDE, 2, NSA_KV_GROUPS, HEAD_DIM).transpose(0, 3, 4, 1, 2, 5)
    chunks = chunks.reshape(B, 4, ncp, CMP_STRIDE * HEAD_DIM)
    nxt = jnp.concatenate([chunks[:, :, 1:], chunks[:, :, :1]], axis=2)
    blocks = jnp.concatenate([chunks, nxt], axis=-1)
    pe = jnp.stack([pe_k.reshape(1, -1), pe_v.reshape(1, -1)])
    w1 = jnp.stack([w1_k, w1_v]).astype(BF16)
    w2d = jnp.stack([jnp.tile(w2_k, (1, 2)), jnp.tile(w2_v, (1, 2))]).astype(BF16)
    cmp4 = _nsa_compress(blocks, pe, w1, w2d, cosc, sinc)
    cmpkv = jnp.concatenate([cmp4[:, :, 0:128], cmp4[:, :, 256:384], cmp4[:, :, 128:256], cmp4[:, :, 384:512]], axis=-1)
    o = _nsa_attention(q.reshape(B, S, 1024), kv.reshape(B, S, 1024), cmpkv,
                       gate.reshape(B, S, 256), agg, emat, S)
    return _out_proj(o.reshape(B * S, 1024), w_out.astype(BF16), x2d, g_post)


def _dsa_mixer(x2d, g_pre, g_post, w_in, kv_norm, w_uk, w_uv, w_out, B, S, tabs):
    cos2, sin2 = tabs[0], tabs[1]
    wukv = jnp.concatenate([w_uk, w_uv], axis=1).astype(BF16)
    q, kv, qi, ki, wi = _dsa_in_proj(x2d, g_pre, _pad_cols(w_in, IN_W_PAD).astype(BF16),
                                     kv_norm.reshape(1, -1), wukv, cos2, sin2, S)
    o = _dsa_attention(q.reshape(B, S, 1024), kv.reshape(B, S, 256), qi.reshape(B, S, 512),
                       ki.reshape(B, S, LANES), wi.reshape(B, S, LANES), S)
    return _out_proj(o.reshape(B * S, 1024), w_out.astype(BF16), x2d, g_post)


def kernel(x, norm_g, nsa_w_in, nsa_cmp_pe_k, nsa_cmp_w1_k, nsa_cmp_w2_k, nsa_cmp_pe_v, nsa_cmp_w1_v, nsa_cmp_w2_v, nsa_w_out, dsa_w_in, dsa_kv_norm, dsa_w_uk, dsa_w_uv, dsa_w_out, ffn_w_up, ffn_conv_w, ffn_conv_b, ffn_w_down):
    B, S, _ = x.shape
    ncp = S // CMP_STRIDE
    n_sel = S // SEL_BLOCK
    cos2, sin2 = _rope_tables(jnp.arange(S, dtype=jnp.int32))
    cosk, sink = _rope_tables(jnp.arange(ncp, dtype=jnp.int32) * CMP_STRIDE + (CMP_BLOCK - 1))
    cosc = jnp.stack([cosk, jnp.ones_like(cosk)])
    sinc = jnp.stack([sink, jnp.zeros_like(sink)])
    ci = np.arange(ncp)[:, None]
    sj = np.arange(LANES)[None, :]
    agg = ((ci * CMP_STRIDE <= sj * SEL_BLOCK + SEL_BLOCK - 1)
           & (ci * CMP_STRIDE + CMP_BLOCK - 1 >= sj * SEL_BLOCK)
           & (ci < (S - CMP_BLOCK) // CMP_STRIDE + 1) & (sj < n_sel))
    agg = jnp.asarray(agg, dtype=BF16)
    emat = jnp.asarray(np.arange(LANES)[:, None] == (np.arange(S)[None, :] // SEL_BLOCK), dtype=BF16)
    tabs = (cos2, sin2, cosc, sinc, agg, emat)

    x2d = x.reshape(B * S, D_MODEL)
    for i in range(DEPTH):
        g = norm_g[i].reshape(4, 1, D_MODEL)
        j = i // 2
        if i % 2 == 0:
            x2d = _nsa_mixer(x2d, g[0], g[1], nsa_w_in[j], nsa_cmp_pe_k[j], nsa_cmp_w1_k[j], nsa_cmp_w2_k[j],
                             nsa_cmp_pe_v[j], nsa_cmp_w1_v[j], nsa_cmp_w2_v[j], nsa_w_out[j], B, S, tabs)
        else:
            x2d = _dsa_mixer(x2d, g[0], g[1], dsa_w_in[j], dsa_kv_norm[j], dsa_w_uk[j], dsa_w_uv[j],
                             dsa_w_out[j], B, S, tabs)
        x2d = _ffn(x2d, g[2], g[3], ffn_w_up[i].astype(BF16), ffn_conv_w[i],
                   ffn_conv_b[i].reshape(1, -1), ffn_w_down[i].astype(BF16), S)
    return x2d.reshape(B, S, D_MODEL)
```

```python
import functools

import numpy as np
import jax
import jax.numpy as jnp
from jax import lax
from jax.experimental import pallas as pl
from jax.experimental.pallas import tpu as pltpu

F32 = jnp.float32
BF16 = jnp.bfloat16

D_MODEL = 1024
DEPTH = 4
N_HEADS = 16
HEAD_DIM = 64
ROPE_THETA = 10000.0
RMS_EPS = 1e-6
NEG_INF = -1e30
BIG = 1e30

NSA_KV_GROUPS = 2
CMP_BLOCK = 32
CMP_STRIDE = 16
CMP_HIDDEN = 256
SEL_BLOCK = 64
SEL_TOP_N = 16
WINDOW = 512

DSA_KV_RANK = 256
IDX_HEADS = 8
IDX_DIM = 64
DSA_TOP_K_MAX = 256

D_FF = 2816
CONV_WIDTH = 3

NSA_IN_W = N_HEADS * HEAD_DIM + 6 * NSA_KV_GROUPS * HEAD_DIM + 3 * N_HEADS
DSA_IN_W = N_HEADS * HEAD_DIM + DSA_KV_RANK + IDX_HEADS * IDX_DIM + IDX_DIM + IDX_HEADS

LANES = 128
IN_W_PAD = 1920
TQ = 128
CK = 256
CK_WIDE = 512
VMEM_LIMIT = 56 * 1024 * 1024


def _cparams(sem):
    return pltpu.CompilerParams(dimension_semantics=sem, vmem_limit_bytes=VMEM_LIMIT)


def _rms(x, g):
    return x * lax.rsqrt(jnp.mean(x * x, axis=-1, keepdims=True) + RMS_EPS) * g


def _lane_iota(shape):
    return lax.broadcasted_iota(jnp.int32, shape, len(shape) - 1)


def _rope128(x, cos2, sin2):
    lane = _lane_iota(x.shape)
    first = (lane % 64) < 32
    swapped = jnp.where(first, pltpu.roll(x, 96, axis=1), pltpu.roll(x, 32, axis=1))
    return x * cos2 + swapped * sin2


def _dup_halves(x):
    lo = _lane_iota(x.shape) < 64
    r = pltpu.roll(x, 64, axis=1)
    return jnp.where(lo, x, r), jnp.where(lo, r, x)


def _dot(a, b):
    return jnp.dot(a, b, preferred_element_type=F32)


def _dot_nt(a, b):
    return lax.dot_general(a, b, (((1,), (1,)), ((), ())), preferred_element_type=F32)


def _tree_sum(parts):
    while len(parts) > 1:
        parts = [parts[j] + parts[j + 1] if j + 1 < len(parts) else parts[j] for j in range(0, len(parts), 2)]
    return parts[0]


def _gelu_tanh(x):
    return 0.5 * x * (1.0 + jnp.tanh(0.7978845608028654 * (x + 0.044715 * (x * x * x))))


def _ones_halves(x):
    lo = _lane_iota(x.shape) < 64
    return jnp.where(lo, x, 1.0), jnp.where(lo, 1.0, x)


def _masked_softmax_pv(q_halves, k_of, v_of, bias_of, c_lo, c_hi, s_ref, rmax_ref, acc_ref, ck=CK):
    n_e = len(q_halves)
    rows = q_halves[0].shape[0]
    r = rows // TQ
    for e in range(n_e):
        rmax_ref[e] = jnp.full((rows, LANES), NEG_INF, F32)
        acc_ref[e] = jnp.zeros((rows, LANES), F32)

    def pass1(c, carry):
        off = pl.multiple_of(c * ck, ck)
        k = k_of(off)
        bias = bias_of(off)
        for e in range(n_e):
            for j in range(r):
                rs = slice(j * TQ, (j + 1) * TQ)
                s = _dot_nt(q_halves[e][rs], k) + bias
                s_ref[e, rs, pl.ds(off, ck)] = s
                cols = [s[:, i * LANES:(i + 1) * LANES] for i in range(ck // LANES)]
                rmax_ref[e, rs] = functools.reduce(jnp.maximum, cols, rmax_ref[e, rs])
        return carry

    lax.fori_loop(c_lo, c_hi, pass1, 0)
    for e in range(n_e):
        m = jnp.max(rmax_ref[e], axis=-1, keepdims=True)
        m = jnp.where(m > NEG_INF, m, 0.0)
        rmax_ref[e] = jnp.broadcast_to(m, (rows, LANES))

    def pass2(c, carry):
        off = pl.multiple_of(c * ck, ck)
        for e in range(n_e):
            v = v_of(e, off)
            for j in range(r):
                rs = slice(j * TQ, (j + 1) * TQ)
                mb = rmax_ref[e, rs]
                p = jnp.exp(s_ref[e, rs, pl.ds(off, ck)] - jnp.concatenate([mb] * (ck // LANES), axis=1))
                acc_ref[e, rs] += _dot(p.astype(BF16), v)
        return carry

    lax.fori_loop(c_lo, c_hi, pass2, 0)


def _norm_pair(acc_a, acc_b, lo):
    num = jnp.where(lo, acc_a, acc_b)
    den = jnp.where(lo, pltpu.roll(acc_a, 64, axis=1), pltpu.roll(acc_b, 64, axis=1))
    return jnp.where(den > 0.0, num / jnp.where(den > 0.0, den, 1.0), 0.0)


def _nsa_in_kernel(x_ref, g_ref, w_ref, cos_ref, sin_ref, q_ref, kv_ref, cv_ref, gate_ref):
    h = _rms(x_ref[...], g_ref[...]).astype(BF16)
    proj = _dot(h, w_ref[...])
    cos2 = cos_ref[...]
    sin2 = sin_ref[...]
    for m in range(8):
        slab = proj[:, m * LANES:(m + 1) * LANES]
        q_ref[:, m * LANES:(m + 1) * LANES] = (_rope128(slab, cos2, sin2) * (HEAD_DIM ** -0.5)).astype(BF16)
    cv_ref[...] = proj[:, 1024:1280]
    ks = _rope128(proj[:, 1280:1408], cos2, sin2)
    vs = proj[:, 1408:1536]
    kw = _rope128(proj[:, 1536:1664], cos2, sin2)
    vw = proj[:, 1664:1792]
    for j, (kk, vv) in enumerate(((ks, vs), (kw, vw))):
        k0, k1 = _dup_halves(kk)
        v0, v1 = _dup_halves(vv)
        for g, (kd, vd) in enumerate(((k0, v0), (k1, v1))):
            base = (g * 6 + j * 3) * LANES
            v_lo, v_hi = _ones_halves(vd)
            kv_ref[:, base:base + LANES] = kd.astype(BF16)
            kv_ref[:, base + LANES:base + 2 * LANES] = v_lo.astype(BF16)
            kv_ref[:, base + 2 * LANES:base + 3 * LANES] = v_hi.astype(BF16)
    gt = proj[:, 1792:1920]
    sg = 1.0 / (1.0 + jnp.exp(-gt))
    gate_ref[:, 0:LANES] = sg
    gate_ref[:, LANES:2 * LANES] = pltpu.roll(sg, LANES - 24, axis=1)


def _nsa_in_proj(x2d, g, w, cos2, sin2, S):
    T = x2d.shape[0]
    tm = 256
    tps = S // tm
    return pl.pallas_call(
        _nsa_in_kernel,
        grid=(T // tm,),
        in_specs=[
            pl.BlockSpec((tm, D_MODEL), lambda i: (i, 0)),
            pl.BlockSpec((1, D_MODEL), lambda i: (0, 0)),
            pl.BlockSpec((D_MODEL, IN_W_PAD), lambda i: (0, 0)),
            pl.BlockSpec((tm, LANES), lambda i: (i % tps, 0)),
            pl.BlockSpec((tm, LANES), lambda i: (i % tps, 0)),
        ],
        out_specs=[
            pl.BlockSpec((tm, 1024), lambda i: (i, 0)),
            pl.BlockSpec((tm, 12 * LANES), lambda i: (i, 0)),
            pl.BlockSpec((tm, 256), lambda i: (i, 0)),
            pl.BlockSpec((tm, 256), lambda i: (i, 0)),
        ],
        out_shape=[
            jax.ShapeDtypeStruct((T, 1024), BF16),
            jax.ShapeDtypeStruct((T, 12 * LANES), BF16),
            jax.ShapeDtypeStruct((T, 256), F32),
            jax.ShapeDtypeStruct((T, 256), F32),
        ],
        compiler_params=_cparams(("parallel",)),
        name="nsa_in_proj",
    )(x2d, g, w, cos2, sin2)


def _nsa_cmp_kernel(blk_ref, pe_ref, w1_ref, w2_ref, cos_ref, sin_ref, o_ref):
    xb = (blk_ref[0, 0] + pe_ref[0]).astype(BF16)
    hid = _gelu_tanh(_dot(xb, w1_ref[0]))
    out = _dot(hid.astype(BF16), w2_ref[0])
    o_ref[0] = _rope128(out, cos_ref[0], sin_ref[0]).astype(BF16)


def _nsa_compress(blocks, pe, w1, w2d, cosc, sinc):
    B, _, ncp, width = blocks.shape
    return pl.pallas_call(
        _nsa_cmp_kernel,
        grid=(B, 4),
        in_specs=[
            pl.BlockSpec((1, 1, ncp, width), lambda b, j: (b, j, 0, 0)),
            pl.BlockSpec((1, 1, width), lambda b, j: (j // 2, 0, 0)),
            pl.BlockSpec((1, width, CMP_HIDDEN), lambda b, j: (j // 2, 0, 0)),
            pl.BlockSpec((1, CMP_HIDDEN, LANES), lambda b, j: (j // 2, 0, 0)),
            pl.BlockSpec((1, ncp, LANES), lambda b, j: (j // 2, 0, 0)),
            pl.BlockSpec((1, ncp, LANES), lambda b, j: (j // 2, 0, 0)),
        ],
        out_specs=pl.BlockSpec((1, ncp, LANES), lambda b, j: (b, 0, j)),
        out_shape=jax.ShapeDtypeStruct((B, ncp, 4 * LANES), BF16),
        compiler_params=_cparams(("parallel", "parallel")),
        name="nsa_compress",
    )(blocks, pe, w1, w2d, cosc, sinc)


def _nsa_attn_kernel(q_ref, kv_ref, cmp_ref, gate_ref, agg_ref, e_ref, o_ref,
                     selk_ref, s_ref, rmax_ref, acc_ref, *, S):
    n_sel = S // SEL_BLOCK
    top_n = min(SEL_TOP_N, n_sel)
    i = pl.program_id(2)
    s0 = i * TQ
    lo = _lane_iota((1, LANES)) < 64
    q = q_ref[0]
    q2 = jnp.concatenate([q[:, p * LANES:(p + 1) * LANES] for p in range(4)], axis=0)
    zero = jnp.zeros_like(q2)
    q_lo = jnp.where(lo, q2, zero)
    q_hi = jnp.where(lo, zero, q2)
    t = s0 + lax.broadcasted_iota(jnp.int32, (TQ, 1), 0)

    kc = cmp_ref[0, :, 0:LANES]
    vc = cmp_ref[0, :, LANES:2 * LANES]
    ncp = kc.shape[0]
    cend = _lane_iota((1, ncp)) * CMP_STRIDE + (CMP_BLOCK - 1)
    mask_c = (cend <= t)[None]

    def cmp_probs(qh):
        s3 = jnp.where(mask_c, _dot_nt(qh, kc).reshape(4, TQ, ncp), NEG_INF)
        e = jnp.where(mask_c, jnp.exp(s3 - jnp.max(s3, axis=-1, keepdims=True)), 0.0)
        l = jnp.sum(e, axis=-1, keepdims=True)
        return e / jnp.where(l > 0.0, l, 1.0)

    pa3 = cmp_probs(q_lo)
    pb3 = cmp_probs(q_hi)
    oc_a = _dot(pa3.reshape(4 * TQ, ncp).astype(BF16), vc)
    oc_b = _dot(pb3.reshape(4 * TQ, ncp).astype(BF16), vc)
    psum = jnp.sum(pa3, axis=0) + jnp.sum(pb3, axis=0)
    agg = agg_ref[...]
    p_hi = psum.astype(BF16)
    r1 = psum - p_hi.astype(F32)
    p_mid = r1.astype(BF16)
    p_lo = (r1 - p_mid.astype(F32)).astype(BF16)
    imp = _dot_nt(agg, p_hi) + _dot_nt(agg, p_mid) + _dot_nt(agg, p_lo)

    tq = s0 + _lane_iota((1, TQ))
    jblk = lax.broadcasted_iota(jnp.int32, (n_sel, 1), 0)
    cur = jnp.right_shift(tq, SEL_BLOCK.bit_length() - 1)
    forced = (jblk == 0) | (jblk == cur) | (jblk == cur - 1)
    causal_blk = jblk * SEL_BLOCK <= tq
    impm = jnp.where(causal_blk, jnp.where(forced, BIG, imp), NEG_INF)
    rank = jnp.zeros((n_sel, TQ), F32)
    for k in range(n_sel):
        row = impm[k:k + 1, :]
        tie_k = jnp.where(jblk > k, 1.0, 0.0)
        rank = rank + jnp.where(row > impm, 1.0, 0.0) + jnp.where(row == impm, tie_k, 0.0)
    sel_t = jnp.where(rank < float(top_n), 1.0, 0.0)
    sel = jnp.concatenate([sel_t, jnp.zeros((LANES - n_sel, TQ), F32)], axis=0).T.astype(BF16)
    selk_ref[...] = _dot(sel, e_ref[...])
    q_halves = [q_lo, q_hi]
    c_hi = s0 // CK + 1

    def sel_bias(off):
        kpos = off + _lane_iota((1, CK_WIDE))
        ok = (selk_ref[:, pl.ds(off, CK_WIDE)] > 0.5) & (kpos <= t)
        return jnp.where(ok, 0.0, NEG_INF)

    _masked_softmax_pv(q_halves,
                       lambda off: kv_ref[0, pl.ds(off, CK_WIDE), 0:LANES],
                       lambda e, off: kv_ref[0, pl.ds(off, CK_WIDE), (1 + e) * LANES:(2 + e) * LANES],
                       sel_bias, 0, s0 // CK_WIDE + 1, s_ref, rmax_ref, acc_ref, ck=CK_WIDE)
    os2 = _norm_pair(acc_ref[0], acc_ref[1], lo)

    def win_bias(off):
        dist = t - (off + _lane_iota((1, CK)))
        return jnp.where((dist >= 0) & (dist < WINDOW), 0.0, NEG_INF)

    c_lo = jnp.maximum(s0 - (WINDOW - 1), 0) // CK
    _masked_softmax_pv(q_halves,
                       lambda off: kv_ref[0, pl.ds(off, CK), 3 * LANES:4 * LANES],
                       lambda e, off: kv_ref[0, pl.ds(off, CK), (4 + e) * LANES:(5 + e) * LANES],
                       win_bias, c_lo, c_hi, s_ref, rmax_ref, acc_ref)
    ow2 = _norm_pair(acc_ref[0], acc_ref[1], lo)

    gt = gate_ref[0]
    for p in range(4):
        rs = slice(p * TQ, (p + 1) * TQ)
        ca, cb = (2 * p) * 3, (2 * p + 1) * 3
        g_c = jnp.where(lo, gt[:, ca:ca + 1], gt[:, cb:cb + 1])
        g_s = jnp.where(lo, gt[:, ca + 1:ca + 2], gt[:, cb + 1:cb + 2])
        g_w = jnp.where(lo, gt[:, ca + 2:ca + 3], gt[:, cb + 2:cb + 3])
        oc2 = jnp.where(lo, oc_a[rs], oc_b[rs])
        o_ref[0, :, p * LANES:(p + 1) * LANES] = (g_c * oc2 + g_s * os2[rs] + g_w * ow2[rs]).astype(BF16)


def _nsa_attention(q, kv, cmpkv, gate, agg, emat, S):
    B = q.shape[0]
    ncp = cmpkv.shape[1]
    return pl.pallas_call(
        functools.partial(_nsa_attn_kernel, S=S),
        grid=(B, NSA_KV_GROUPS, S // TQ),
        in_specs=[
            pl.BlockSpec((1, TQ, 512), lambda b, g, i: (b, i, g)),
            pl.BlockSpec((1, S, 6 * LANES), lambda b, g, i: (b, 0, g)),
            pl.BlockSpec((1, ncp, 2 * LANES), lambda b, g, i: (b, 0, g)),
            pl.BlockSpec((1, TQ, LANES), lambda b, g, i: (b, i, g)),
            pl.BlockSpec((S // SEL_BLOCK, ncp), lambda b, g, i: (0, 0)),
            pl.BlockSpec((LANES, S), lambda b, g, i: (0, 0)),
        ],
        out_specs=pl.BlockSpec((1, TQ, 512), lambda b, g, i: (b, i, g)),
        out_shape=jax.ShapeDtypeStruct((B, S, 1024), BF16),
        scratch_shapes=[pltpu.VMEM((TQ, S), F32),
                        pltpu.VMEM((2, 4 * TQ, S), F32),
                        pltpu.VMEM((2, 4 * TQ, LANES), F32),
                        pltpu.VMEM((2, 4 * TQ, LANES), F32)],
        compiler_params=_cparams(("parallel", "parallel", "parallel")),
        name="nsa_attention",
    )(q, kv, cmpkv, gate, agg, emat)


def _dsa_in_kernel(x_ref, g_ref, w_ref, kvn_ref, wukv_ref, cos_ref, sin_ref,
                   q_ref, kv_ref, qi_ref, ki_ref, wi_ref):
    h = _rms(x_ref[...], g_ref[...]).astype(BF16)
    proj = _dot(h, w_ref[...])
    cos2 = cos_ref[...]
    sin2 = sin_ref[...]
    for m in range(8):
        slab = proj[:, m * LANES:(m + 1) * LANES]
        q_ref[:, m * LANES:(m + 1) * LANES] = (_rope128(slab, cos2, sin2) * (HEAD_DIM ** -0.5)).astype(BF16)
    c = _rms(proj[:, 1024:1280], kvn_ref[...]).astype(BF16)
    kvu = _dot(c, wukv_ref[...])
    kd, vd = _dup_halves(kvu)
    v_lo, v_hi = _ones_halves(vd)
    kv_ref[:, 0:LANES] = _rope128(kd, cos2, sin2).astype(BF16)
    kv_ref[:, LANES:2 * LANES] = v_lo.astype(BF16)
    kv_ref[:, 2 * LANES:3 * LANES] = v_hi.astype(BF16)
    for m in range(4):
        slab = proj[:, 1280 + m * LANES:1280 + (m + 1) * LANES]
        qi_ref[:, m * LANES:(m + 1) * LANES] = (_rope128(slab, cos2, sin2) * (IDX_DIM ** -0.5)).astype(BF16)
    last = proj[:, 1792:1920]
    kid, wid = _dup_halves(last)
    ki_ref[...] = _rope128(kid, cos2, sin2).astype(BF16)
    wi_ref[0] = (wid * (IDX_HEADS ** -0.5)).T[0:IDX_HEADS, :]


def _dsa_in_proj(x2d, g, w, kvn, wukv, cos2, sin2, S):
    T = x2d.shape[0]
    tm = 256
    tps = S // tm
    return pl.pallas_call(
        _dsa_in_kernel,
        grid=(T // tm,),
        in_specs=[
            pl.BlockSpec((tm, D_MODEL), lambda i: (i, 0)),
            pl.BlockSpec((1, D_MODEL), lambda i: (0, 0)),
            pl.BlockSpec((D_MODEL, IN_W_PAD), lambda i: (0, 0)),
            pl.BlockSpec((1, DSA_KV_RANK), lambda i: (0, 0)),
            pl.BlockSpec((DSA_KV_RANK, LANES), lambda i: (0, 0)),
            pl.BlockSpec((tm, LANES), lambda i: (i % tps, 0)),
            pl.BlockSpec((tm, LANES), lambda i: (i % tps, 0)),
        ],
        out_specs=[
            pl.BlockSpec((tm, 1024), lambda i: (i, 0)),
            pl.BlockSpec((tm, 3 * LANES), lambda i: (i, 0)),
            pl.BlockSpec((tm, 512), lambda i: (i, 0)),
            pl.BlockSpec((tm, LANES), lambda i: (i, 0)),
            pl.BlockSpec((1, IDX_HEADS, tm), lambda i: (i // tps, 0, i % tps)),
        ],
        out_shape=[
            jax.ShapeDtypeStruct((T, 1024), BF16),
            jax.ShapeDtypeStruct((T, 3 * LANES), BF16),
            jax.ShapeDtypeStruct((T, 512), BF16),
            jax.ShapeDtypeStruct((T, LANES), BF16),
            jax.ShapeDtypeStruct((T // S, IDX_HEADS, S), F32),
        ],
        compiler_params=_cparams(("parallel",)),
        name="dsa_in_proj",
    )(x2d, g, w, kvn, wukv, cos2, sin2)


def _sortable(v):
    bits = lax.bitcast_convert_type(v + 0.0, jnp.int32)
    return bits ^ ((bits >> 31) & jnp.int32(0x7FFFFFFF))


_NEG_BITS = np.array([NEG_INF], np.float32).view(np.int32)
_NEG_KEY = int((_NEG_BITS ^ ((_NEG_BITS >> 31) & np.int32(0x7FFFFFFF)))[0])
_INT_MIN = -(2 ** 31)


def _dsa_attn_kernel(q_ref, kv_ref, qi_ref, ki_ref, wi_ref, o_ref,
                     key_ref, bias_ref, s_ref, rmax_ref, acc_ref, *, S):
    top_k = min(DSA_TOP_K_MAX, S // 4)
    idx_bits = (S - 1).bit_length()
    i = pl.program_id(1)
    s0 = i * TQ
    n_chunks = s0 // CK + 1
    lo = _lane_iota((1, LANES)) < 64
    tq = s0 + _lane_iota((1, TQ))

    qi = qi_ref[0]
    qi2 = jnp.concatenate([qi[:, p * LANES:(p + 1) * LANES] for p in range(4)], axis=0)
    zi = jnp.zeros_like(qi2)
    qi_lo = jnp.where(lo, qi2, zi)
    qi_hi = jnp.where(lo, zi, qi2)
    wt = wi_ref[0]

    def idx_body(c, carry):
        off = pl.multiple_of(c * CK, CK)
        ki = ki_ref[0, pl.ds(off, CK), :]
        sa = _dot_nt(ki, qi_lo)
        sb = _dot_nt(ki, qi_hi)
        score = jnp.zeros((CK, TQ), F32)
        for p in range(4):
            cs = slice(p * TQ, (p + 1) * TQ)
            score = score + jnp.maximum(sa[:, cs], 0.0) * wt[2 * p:2 * p + 1, :]
            score = score + jnp.maximum(sb[:, cs], 0.0) * wt[2 * p + 1:2 * p + 2, :]
        kpos = off + lax.broadcasted_iota(jnp.int32, (CK, 1), 0)
        score = jnp.where(kpos <= tq, score, NEG_INF)
        key_ref[pl.ds(off, CK), :] = _sortable(score)
        return carry

    lax.fori_loop(0, n_chunks, idx_body, 0)

    n_out = (S - n_chunks * CK).astype(F32)

    def count(pred_fn):
        def body(c, acc):
            off = pl.multiple_of(c * CK, CK)
            hit = jnp.where(pred_fn(key_ref[pl.ds(off, CK), :], off), 1.0, 0.0)
            return acc + _tree_sum([hit[8 * j:8 * j + 8] for j in range(CK // 8)])
        acc = lax.fori_loop(0, n_chunks, body, jnp.zeros((8, TQ), F32))
        return jnp.sum(acc, axis=0, keepdims=True)

    def bit_body(it, tu):
        bit = jnp.left_shift(jnp.int32(1), 31 - it)
        cand = (tu | bit) ^ jnp.int32(_INT_MIN)
        cnt = count(lambda kk, off: kk >= cand)
        cnt = cnt + n_out * jnp.where(jnp.int32(_NEG_KEY) >= cand, 1.0, 0.0)
        return jnp.where(cnt >= float(top_k), tu | bit, tu)

    tu = lax.fori_loop(0, 32, bit_body, jnp.zeros((1, TQ), jnp.int32))
    thr = tu ^ jnp.int32(_INT_MIN)
    n_gt = count(lambda kk, off: kk > thr)
    n_gt = n_gt + n_out * jnp.where(jnp.int32(_NEG_KEY) > thr, 1.0, 0.0)
    need = float(top_k) - n_gt
    n_eq = count(lambda kk, off: kk == thr)

    def row_iota(off):
        return off + lax.broadcasted_iota(jnp.int32, (CK, 1), 0)

    def tie_search():
        def tie_body(it, jb):
            bit = jnp.left_shift(jnp.int32(1), idx_bits - 1 - it)
            cand = jb | bit
            cnt = count(lambda kk, off: (kk == thr) & (row_iota(off) < cand))
            return jnp.where(cnt < need, cand, jb)
        return lax.fori_loop(0, idx_bits, tie_body, jnp.zeros((1, TQ), jnp.int32))

    jb = lax.cond(jnp.max(n_eq - need) > 0.0, tie_search,
                  lambda: jnp.full((1, TQ), 2 ** idx_bits - 1, jnp.int32))

    eye = jnp.where(lax.broadcasted_iota(jnp.int32, (TQ, TQ), 0) == _lane_iota((TQ, TQ)), 1.0, 0.0).astype(BF16)

    def bias_body(c, carry):
        off = pl.multiple_of(c * CK, CK)
        kk = key_ref[pl.ds(off, CK), :]
        kpos = row_iota(off)
        picked = ((kk > thr) | ((kk == thr) & (kpos <= jb))) & (kpos <= tq)
        sel_qk = _dot_nt(eye, jnp.where(picked, 1.0, 0.0).astype(BF16))
        bias_ref[:, pl.ds(off, CK)] = (sel_qk - 1.0) * BIG
        return carry

    lax.fori_loop(0, n_chunks, bias_body, 0)

    @pl.when(n_chunks * CK < (s0 // CK_WIDE + 1) * CK_WIDE)
    def _():
        bias_ref[:, pl.ds(pl.multiple_of(n_chunks * CK, CK), CK)] = jnp.full((TQ, CK), NEG_INF, F32)

    q = q_ref[0]
    q2 = jnp.concatenate([q[:, p * LANES:(p + 1) * LANES] for p in range(8)], axis=0)
    zq = jnp.zeros_like(q2)
    for e, qh in enumerate((jnp.where(lo, q2, zq), jnp.where(lo, zq, q2))):
        _masked_softmax_pv([qh],
                           lambda off: kv_ref[0, pl.ds(off, CK_WIDE), 0:LANES],
                           lambda _, off, e=e: kv_ref[0, pl.ds(off, CK_WIDE), (1 + e) * LANES:(2 + e) * LANES],
                           lambda off: bias_ref[:, pl.ds(off, CK_WIDE)],
                           0, s0 // CK_WIDE + 1, s_ref, rmax_ref, acc_ref.at[pl.ds(e, 1)], ck=CK_WIDE)
    o2 = _norm_pair(acc_ref[0], acc_ref[1], lo)
    for p in range(8):
        o_ref[0, :, p * LANES:(p + 1) * LANES] = o2[p * TQ:(p + 1) * TQ].astype(BF16)


def _dsa_attention(q, kv, qi, ki, wi, S):
    B = q.shape[0]
    return pl.pallas_call(
        functools.partial(_dsa_attn_kernel, S=S),
        grid=(B, S // TQ),
        in_specs=[
            pl.BlockSpec((1, TQ, 1024), lambda b, i: (b, i, 0)),
            pl.BlockSpec((1, S, 3 * LANES), lambda b, i: (b, 0, 0)),
            pl.BlockSpec((1, TQ, 512), lambda b, i: (b, i, 0)),
            pl.BlockSpec((1, S, LANES), lambda b, i: (b, 0, 0)),
            pl.BlockSpec((1, IDX_HEADS, TQ), lambda b, i: (b, 0, i)),
        ],
        out_specs=pl.BlockSpec((1, TQ, 1024), lambda b, i: (b, i, 0)),
        out_shape=jax.ShapeDtypeStruct((B, S, 1024), BF16),
        scratch_shapes=[pltpu.VMEM((S, TQ), jnp.int32),
                        pltpu.VMEM((TQ, S), F32),
                        pltpu.VMEM((1, 8 * TQ, S), F32),
                        pltpu.VMEM((1, 8 * TQ, LANES), F32),
                        pltpu.VMEM((2, 8 * TQ, LANES), F32)],
        compiler_params=_cparams(("parallel", "parallel")),
        name="dsa_attention",
    )(q, kv, qi, ki, wi)


def _out_proj_kernel(a_ref, w_ref, x_ref, g_ref, o_ref):
    y = _dot(a_ref[...], w_ref[...])
    o_ref[...] = x_ref[...] + _rms(y, g_ref[...])


def _out_proj(a2d, w, x2d, g):
    T = x2d.shape[0]
    tm = 512
    return pl.pallas_call(
        _out_proj_kernel,
        grid=(T // tm,),
        in_specs=[
            pl.BlockSpec((tm, 1024), lambda i: (i, 0)),
            pl.BlockSpec((1024, D_MODEL), lambda i: (0, 0)),
            pl.BlockSpec((tm, D_MODEL), lambda i: (i, 0)),
            pl.BlockSpec((1, D_MODEL), lambda i: (0, 0)),
        ],
        out_specs=pl.BlockSpec((tm, D_MODEL), lambda i: (i, 0)),
        out_shape=jax.ShapeDtypeStruct((T, D_MODEL), F32),
        compiler_params=_cparams(("parallel",)),
        name="mixer_out_proj",
    )(a2d, w, x2d, g)


FFN_TM = 512
FFN_FC = 256
FFN_NF = D_FF // FFN_FC
FFN_HALO = 16


def _ffn_kernel(x_ref, xh_ref, g2_ref, g3_ref, wv_ref, wg_ref, cwv_ref, cwg_ref, cbv_ref, cbg_ref,
                wd_ref, o_ref, hext, uv, ug, acc, *, tiles_per_seq):
    i = pl.program_id(0)
    f = pl.program_id(1)
    tm = FFN_TM

    @pl.when(f == 0)
    def _():
        g2 = g2_ref[...]
        hext[FFN_HALO:, :] = _rms(x_ref[...], g2).astype(BF16)
        hh = _rms(xh_ref[...], g2)
        hh = jnp.where(i % tiles_per_seq == 0, 0.0, hh)
        hext[0:FFN_HALO, :] = hh.astype(BF16)
        acc[...] = jnp.zeros(acc.shape, F32)

    he = hext[...]
    uv[...] = _dot(he, wv_ref[...])
    ug[...] = _dot(he, wg_ref[...])

    def conv(u_ref, cw_ref, cb_ref):
        w = cw_ref[...]
        return (w[2:3] * u_ref[FFN_HALO:FFN_HALO + tm, :]
                + w[1:2] * u_ref[FFN_HALO - 1:FFN_HALO - 1 + tm, :]
                + w[0:1] * u_ref[FFN_HALO - 2:FFN_HALO - 2 + tm, :]
                + cb_ref[...])

    val = conv(uv, cwv_ref, cbv_ref)
    gate = conv(ug, cwg_ref, cbg_ref)
    act = (_gelu_tanh(gate) * val).astype(BF16)
    acc[...] += _dot(act, wd_ref[...])

    @pl.when(f == FFN_NF - 1)
    def _():
        o_ref[...] = x_ref[...] + _rms(acc[...], g3_ref[...])


def _ffn(x2d, g2, g3, wup, cw, cb, wd, S):
    T = x2d.shape[0]
    tm = FFN_TM
    hb = tm // FFN_HALO
    return pl.pallas_call(
        functools.partial(_ffn_kernel, tiles_per_seq=S // tm),
        grid=(T // tm, FFN_NF),
        in_specs=[
            pl.BlockSpec((tm, D_MODEL), lambda i, f: (i, 0)),
            pl.BlockSpec((FFN_HALO, D_MODEL), lambda i, f: (jnp.maximum(i * hb - 1, 0), 0)),
            pl.BlockSpec((1, D_MODEL), lambda i, f: (0, 0)),
            pl.BlockSpec((1, D_MODEL), lambda i, f: (0, 0)),
            pl.BlockSpec((D_MODEL, FFN_FC), lambda i, f: (0, f)),
            pl.BlockSpec((D_MODEL, FFN_FC), lambda i, f: (0, FFN_NF + f)),
            pl.BlockSpec((CONV_WIDTH, FFN_FC), lambda i, f: (0, f)),
            pl.BlockSpec((CONV_WIDTH, FFN_FC), lambda i, f: (0, FFN_NF + f)),
            pl.BlockSpec((1, FFN_FC), lambda i, f: (0, f)),
            pl.BlockSpec((1, FFN_FC), lambda i, f: (0, FFN_NF + f)),
            pl.BlockSpec((FFN_FC, D_MODEL), lambda i, f: (f, 0)),
        ],
        out_specs=pl.BlockSpec((tm, D_MODEL), lambda i, f: (i, 0)),
        out_shape=jax.ShapeDtypeStruct((T, D_MODEL), F32),
        scratch_shapes=[
            pltpu.VMEM((tm + FFN_HALO, D_MODEL), BF16),
            pltpu.VMEM((tm + FFN_HALO, FFN_FC), F32),
            pltpu.VMEM((tm + FFN_HALO, FFN_FC), F32),
            pltpu.VMEM((tm, D_MODEL), F32),
        ],
        compiler_params=_cparams(("parallel", "arbitrary")),
        name="conv_ffn",
    )(x2d, x2d, g2, g3, wup, wup, cw, cw, cb, cb, wd)


def _rope_tables(pos):
    half = HEAD_DIM // 2
    inv = ROPE_THETA ** (-jnp.arange(half, dtype=F32) / half)
    ang = pos.astype(F32)[:, None] * inv[None, :]
    cos, sin = jnp.cos(ang), jnp.sin(ang)
    return jnp.tile(cos, (1, 4)), jnp.tile(jnp.concatenate([-sin, sin], axis=-1), (1, 2))


def _tables(S):
    ncp = S // CMP_STRIDE
    n_sel = S // SEL_BLOCK
    cos2, sin2 = _rope_tables(jnp.arange(S, dtype=jnp.int32))
    cosk, sink = _rope_tables(jnp.arange(ncp, dtype=jnp.int32) * CMP_STRIDE + (CMP_BLOCK - 1))
    cosc = jnp.stack([cosk, jnp.ones_like(cosk)])
    sinc = jnp.stack([sink, jnp.zeros_like(sink)])
    sj = np.arange(n_sel)[:, None]
    ci = np.arange(ncp)[None, :]
    agg_t = ((ci * CMP_STRIDE <= sj * SEL_BLOCK + SEL_BLOCK - 1)
             & (ci * CMP_STRIDE + CMP_BLOCK - 1 >= sj * SEL_BLOCK)
             & (ci < (S - CMP_BLOCK) // CMP_STRIDE + 1))
    agg_t = jnp.asarray(agg_t, dtype=BF16)
    emat = jnp.asarray(np.arange(LANES)[:, None] == (np.arange(S)[None, :] // SEL_BLOCK), dtype=BF16)
    return (cos2, sin2, cosc, sinc, agg_t, emat)


def _pad_cols(w, width):
    return jnp.pad(w, ((0, 0), (0, width - w.shape[1])))


def _nsa_mixer(x2d, g_pre, g_post, w_in, pe_k, w1_k, w2_k, pe_v, w1_v, w2_v, w_out, B, S, tabs):
    cos2, sin2, cosc, sinc, agg, emat = tabs
    ncp = S // CMP_STRIDE
    q, kv, cv, gate = _nsa_in_proj(x2d, g_pre, _pad_cols(w_in, IN_W_PAD).astype(BF16), cos2, sin2, S)
    chunks = cv.reshape(B, ncp, CMP_STRIDE, 2, NSA_KV_GROUPS, HEAD_DIM).transpose(0, 3, 4, 1, 2, 5)
    chunks = chunks.reshape(B, 4, ncp, CMP_STRIDE * HEAD_DIM)
    nxt = jnp.concatenate([chunks[:, :, 1:], chunks[:, :, :1]], axis=2)
    blocks = jnp.concatenate([chunks, nxt], axis=-1)
    pe = jnp.stack([pe_k.reshape(1, -1), pe_v.reshape(1, -1)])
    w1 = jnp.stack([w1_k, w1_v]).astype(BF16)
    w2d = jnp.stack([jnp.tile(w2_k, (1, 2)), jnp.tile(w2_v, (1, 2))]).astype(BF16)
    cmp4 = _nsa_compress(blocks, pe, w1, w2d, cosc, sinc)
    cmpkv = jnp.concatenate([cmp4[:, :, 0:128], cmp4[:, :, 256:384], cmp4[:, :, 128:256], cmp4[:, :, 384:512]], axis=-1)
    o = _nsa_attention(q.reshape(B, S, 1024), kv.reshape(B, S, 12 * LANES), cmpkv,
                       gate.reshape(B, S, 256), agg, emat, S)
    return _out_proj(o.reshape(B * S, 1024), w_out.astype(BF16), x2d, g_post)


def _dsa_mixer(x2d, g_pre, g_post, w_in, kv_norm, w_uk, w_uv, w_out, B, S, tabs):
    cos2, sin2 = tabs[0], tabs[1]
    wukv = jnp.concatenate([w_uk, w_uv], axis=1).astype(BF16)
    q, kv, qi, ki, wi = _dsa_in_proj(x2d, g_pre, _pad_cols(w_in, IN_W_PAD).astype(BF16),
                                     kv_norm.reshape(1, -1), wukv, cos2, sin2, S)
    o = _dsa_attention(q.reshape(B, S, 1024), kv.reshape(B, S, 3 * LANES), qi.reshape(B, S, 512),
                       ki.reshape(B, S, LANES), wi, S)
    return _out_proj(o.reshape(B * S, 1024), w_out.astype(BF16), x2d, g_post)


def kernel(x, norm_g, nsa_w_in, nsa_cmp_pe_k, nsa_cmp_w1_k, nsa_cmp_w2_k, nsa_cmp_pe_v, nsa_cmp_w1_v, nsa_cmp_w2_v, nsa_w_out, dsa_w_in, dsa_kv_norm, dsa_w_uk, dsa_w_uv, dsa_w_out, ffn_w_up, ffn_conv_w, ffn_conv_b, ffn_w_down):
    B, S, _ = x.shape
    tabs = _tables(S)
    x2d = x.reshape(B * S, D_MODEL)
    for i in range(DEPTH):
        g = norm_g[i].reshape(4, 1, D_MODEL)
        j = i // 2
        if i % 2 == 0:
            x2d = _nsa_mixer(x2d, g[0], g[1], nsa_w_in[j], nsa_cmp_pe_k[j], nsa_cmp_w1_k[j], nsa_cmp_w2_k[j],
                             nsa_cmp_pe_v[j], nsa_cmp_w1_v[j], nsa_cmp_w2_v[j], nsa_w_out[j], B, S, tabs)
        else:
            x2d = _dsa_mixer(x2d, g[0], g[1], dsa_w_in[j], dsa_kv_norm[j], dsa_w_uk[j], dsa_w_uv[j],
                             dsa_w_out[j], B, S, tabs)
        x2d = _ffn(x2d, g[2], g[3], ffn_w_up[i].astype(BF16), ffn_conv_w[i],
                   ffn_conv_b[i].reshape(1, -1), ffn_w_down[i].astype(BF16), S)
    return x2d.reshape(B, S, D_MODEL)
```

```python
import functools

import numpy as np
import jax
import jax.numpy as jnp
from jax import lax
from jax.experimental import pallas as pl
from jax.experimental.pallas import tpu as pltpu

F32 = jnp.float32
BF16 = jnp.bfloat16

D_MODEL = 1024
DEPTH = 4
N_HEADS = 16
HEAD_DIM = 64
ROPE_THETA = 10000.0
RMS_EPS = 1e-6
NEG_INF = -1e30
BIG = 1e30

NSA_KV_GROUPS = 2
CMP_BLOCK = 32
CMP_STRIDE = 16
CMP_HIDDEN = 256
SEL_BLOCK = 64
SEL_TOP_N = 16
WINDOW = 512

DSA_KV_RANK = 256
IDX_HEADS = 8
IDX_DIM = 64
DSA_TOP_K_MAX = 256

D_FF = 2816
CONV_WIDTH = 3

NSA_IN_W = N_HEADS * HEAD_DIM + 6 * NSA_KV_GROUPS * HEAD_DIM + 3 * N_HEADS
DSA_IN_W = N_HEADS * HEAD_DIM + DSA_KV_RANK + IDX_HEADS * IDX_DIM + IDX_DIM + IDX_HEADS

LANES = 128
IN_W_PAD = 1920
TQ = 256
CK = 256
CK_WIDE = 512
VMEM_LIMIT = 56 * 1024 * 1024


def _cparams(sem):
    return pltpu.CompilerParams(dimension_semantics=sem, vmem_limit_bytes=VMEM_LIMIT)


def _rms(x, g):
    return x * lax.rsqrt(jnp.mean(x * x, axis=-1, keepdims=True) + RMS_EPS) * g


def _lane_iota(shape):
    return lax.broadcasted_iota(jnp.int32, shape, len(shape) - 1)


def _rope128(x, cos2, sin2):
    lane = _lane_iota(x.shape)
    first = (lane % 64) < 32
    swapped = jnp.where(first, pltpu.roll(x, 96, axis=1), pltpu.roll(x, 32, axis=1))
    return x * cos2 + swapped * sin2


def _dup_halves(x):
    lo = _lane_iota(x.shape) < 64
    r = pltpu.roll(x, 64, axis=1)
    return jnp.where(lo, x, r), jnp.where(lo, r, x)


def _dot(a, b):
    return jnp.dot(a, b, preferred_element_type=F32)


def _dot_nt(a, b):
    return lax.dot_general(a, b, (((1,), (1,)), ((), ())), preferred_element_type=F32)


def _tree_sum(parts):
    while len(parts) > 1:
        parts = [parts[j] + parts[j + 1] if j + 1 < len(parts) else parts[j] for j in range(0, len(parts), 2)]
    return parts[0]


def _gelu_tanh(x):
    return 0.5 * x * (1.0 + jnp.tanh(0.7978845608028654 * (x + 0.044715 * (x * x * x))))


def _ones_halves(x):
    lo = _lane_iota(x.shape) < 64
    return jnp.where(lo, x, 1.0), jnp.where(lo, 1.0, x)


def _masked_softmax_pv(q_halves, k_of, v_of, bias_of, c_lo, c_hi, s_ref, rmax_ref, acc_ref, ck=CK):
    n_e = len(q_halves)
    rows = q_halves[0].shape[0]
    r = rows // TQ
    for e in range(n_e):
        rmax_ref[e] = jnp.full((rows, LANES), NEG_INF, F32)
        acc_ref[e] = jnp.zeros((rows, LANES), F32)

    def pass1(c, carry):
        off = pl.multiple_of(c * ck, ck)
        k = k_of(off)
        bias = bias_of(off)
        for e in range(n_e):
            for j in range(r):
                rs = slice(j * TQ, (j + 1) * TQ)
                s = _dot_nt(q_halves[e][rs], k) + bias
                s_ref[e, rs, pl.ds(off, ck)] = s
                cols = [s[:, i * LANES:(i + 1) * LANES] for i in range(ck // LANES)]
                rmax_ref[e, rs] = functools.reduce(jnp.maximum, cols, rmax_ref[e, rs])
        return carry

    lax.fori_loop(c_lo, c_hi, pass1, 0)
    for e in range(n_e):
        m = jnp.max(rmax_ref[e], axis=-1, keepdims=True)
        m = jnp.where(m > NEG_INF, m, 0.0)
        rmax_ref[e] = jnp.broadcast_to(m, (rows, LANES))

    def pass2(c, carry):
        off = pl.multiple_of(c * ck, ck)
        for e in range(n_e):
            v = v_of(e, off)
            for j in range(r):
                rs = slice(j * TQ, (j + 1) * TQ)
                mb = rmax_ref[e, rs]
                p = jnp.exp(s_ref[e, rs, pl.ds(off, ck)] - jnp.concatenate([mb] * (ck // LANES), axis=1))
                acc_ref[e, rs] += _dot(p.astype(BF16), v)
        return carry

    lax.fori_loop(c_lo, c_hi, pass2, 0)


def _norm_pair(acc_a, acc_b, lo):
    num = jnp.where(lo, acc_a, acc_b)
    den = jnp.where(lo, pltpu.roll(acc_a, 64, axis=1), pltpu.roll(acc_b, 64, axis=1))
    return jnp.where(den > 0.0, num / jnp.where(den > 0.0, den, 1.0), 0.0)


def _nsa_in_kernel(x_ref, g_ref, w_ref, cos_ref, sin_ref, q_ref, kv_ref, cv_ref, gate_ref):
    h = _rms(x_ref[...], g_ref[...]).astype(BF16)
    proj = _dot(h, w_ref[...])
    cos2 = cos_ref[...]
    sin2 = sin_ref[...]
    for m in range(8):
        slab = proj[:, m * LANES:(m + 1) * LANES]
        q_ref[:, m * LANES:(m + 1) * LANES] = (_rope128(slab, cos2, sin2) * (HEAD_DIM ** -0.5)).astype(BF16)
    cv_ref[...] = proj[:, 1024:1280]
    ks = _rope128(proj[:, 1280:1408], cos2, sin2)
    vs = proj[:, 1408:1536]
    kw = _rope128(proj[:, 1536:1664], cos2, sin2)
    vw = proj[:, 1664:1792]
    for j, (kk, vv) in enumerate(((ks, vs), (kw, vw))):
        k0, k1 = _dup_halves(kk)
        v0, v1 = _dup_halves(vv)
        for g, (kd, vd) in enumerate(((k0, v0), (k1, v1))):
            base = (g * 6 + j * 3) * LANES
            v_lo, v_hi = _ones_halves(vd)
            kv_ref[:, base:base + LANES] = kd.astype(BF16)
            kv_ref[:, base + LANES:base + 2 * LANES] = v_lo.astype(BF16)
            kv_ref[:, base + 2 * LANES:base + 3 * LANES] = v_hi.astype(BF16)
    gt = proj[:, 1792:1920]
    sg = 1.0 / (1.0 + jnp.exp(-gt))
    gate_ref[:, 0:LANES] = sg
    gate_ref[:, LANES:2 * LANES] = pltpu.roll(sg, LANES - 24, axis=1)


def _nsa_in_proj(x2d, g, w, cos2, sin2, S):
    T = x2d.shape[0]
    tm = 256
    tps = S // tm
    return pl.pallas_call(
        _nsa_in_kernel,
        grid=(T // tm,),
        in_specs=[
            pl.BlockSpec((tm, D_MODEL), lambda i: (i, 0)),
            pl.BlockSpec((1, D_MODEL), lambda i: (0, 0)),
            pl.BlockSpec((D_MODEL, IN_W_PAD), lambda i: (0, 0)),
            pl.BlockSpec((tm, LANES), lambda i: (i % tps, 0)),
            pl.BlockSpec((tm, LANES), lambda i: (i % tps, 0)),
        ],
        out_specs=[
            pl.BlockSpec((tm, 1024), lambda i: (i, 0)),
            pl.BlockSpec((tm, 12 * LANES), lambda i: (i, 0)),
            pl.BlockSpec((tm, 256), lambda i: (i, 0)),
            pl.BlockSpec((tm, 256), lambda i: (i, 0)),
        ],
        out_shape=[
            jax.ShapeDtypeStruct((T, 1024), BF16),
            jax.ShapeDtypeStruct((T, 12 * LANES), BF16),
            jax.ShapeDtypeStruct((T, 256), F32),
            jax.ShapeDtypeStruct((T, 256), F32),
        ],
        compiler_params=_cparams(("parallel",)),
        name="nsa_in_proj",
    )(x2d, g, w, cos2, sin2)


def _nsa_cmp_kernel(blk_ref, pe_ref, w1_ref, w2_ref, cos_ref, sin_ref, o_ref):
    xb = (blk_ref[0, 0] + pe_ref[0]).astype(BF16)
    hid = _gelu_tanh(_dot(xb, w1_ref[0]))
    out = _dot(hid.astype(BF16), w2_ref[0])
    o_ref[0] = _rope128(out, cos_ref[0], sin_ref[0]).astype(BF16)


def _nsa_compress(blocks, pe, w1, w2d, cosc, sinc):
    B, _, ncp, width = blocks.shape
    return pl.pallas_call(
        _nsa_cmp_kernel,
        grid=(B, 4),
        in_specs=[
            pl.BlockSpec((1, 1, ncp, width), lambda b, j: (b, j, 0, 0)),
            pl.BlockSpec((1, 1, width), lambda b, j: (j // 2, 0, 0)),
            pl.BlockSpec((1, width, CMP_HIDDEN), lambda b, j: (j // 2, 0, 0)),
            pl.BlockSpec((1, CMP_HIDDEN, LANES), lambda b, j: (j // 2, 0, 0)),
            pl.BlockSpec((1, ncp, LANES), lambda b, j: (j // 2, 0, 0)),
            pl.BlockSpec((1, ncp, LANES), lambda b, j: (j // 2, 0, 0)),
        ],
        out_specs=pl.BlockSpec((1, ncp, LANES), lambda b, j: (b, 0, j)),
        out_shape=jax.ShapeDtypeStruct((B, ncp, 4 * LANES), BF16),
        compiler_params=_cparams(("parallel", "parallel")),
        name="nsa_compress",
    )(blocks, pe, w1, w2d, cosc, sinc)


def _nsa_attn_kernel(q_ref, kv_ref, cmp_ref, gate_ref, agg_ref, e_ref, o_ref,
                     selk_ref, s_ref, rmax_ref, acc_ref, *, S):
    n_sel = S // SEL_BLOCK
    top_n = min(SEL_TOP_N, n_sel)
    i = pl.program_id(2)
    s0 = i * TQ
    lo = _lane_iota((1, LANES)) < 64
    q = q_ref[0]
    q2 = jnp.concatenate([q[:, p * LANES:(p + 1) * LANES] for p in range(4)], axis=0)
    zero = jnp.zeros_like(q2)
    q_lo = jnp.where(lo, q2, zero)
    q_hi = jnp.where(lo, zero, q2)
    t = s0 + lax.broadcasted_iota(jnp.int32, (TQ, 1), 0)

    kc = cmp_ref[0, :, 0:LANES]
    vc = cmp_ref[0, :, LANES:2 * LANES]
    ncp = kc.shape[0]
    cend = _lane_iota((1, ncp)) * CMP_STRIDE + (CMP_BLOCK - 1)
    mask_c = (cend <= t)[None]

    def cmp_probs(qh):
        s3 = jnp.where(mask_c, _dot_nt(qh, kc).reshape(4, TQ, ncp), NEG_INF)
        e = jnp.where(mask_c, jnp.exp(s3 - jnp.max(s3, axis=-1, keepdims=True)), 0.0)
        l = jnp.sum(e, axis=-1, keepdims=True)
        return e / jnp.where(l > 0.0, l, 1.0)

    pa3 = cmp_probs(q_lo)
    pb3 = cmp_probs(q_hi)
    oc_a = _dot(pa3.reshape(4 * TQ, ncp).astype(BF16), vc)
    oc_b = _dot(pb3.reshape(4 * TQ, ncp).astype(BF16), vc)
    psum = jnp.sum(pa3, axis=0) + jnp.sum(pb3, axis=0)
    agg = agg_ref[...]
    p_hi = psum.astype(BF16)
    r1 = psum - p_hi.astype(F32)
    p_mid = r1.astype(BF16)
    p_lo = (r1 - p_mid.astype(F32)).astype(BF16)
    imp = _dot_nt(agg, p_hi) + _dot_nt(agg, p_mid) + _dot_nt(agg, p_lo)

    tq = s0 + _lane_iota((1, TQ))
    jblk = lax.broadcasted_iota(jnp.int32, (n_sel, 1), 0)
    cur = jnp.right_shift(tq, SEL_BLOCK.bit_length() - 1)
    forced = (jblk == 0) | (jblk == cur) | (jblk == cur - 1)
    causal_blk = jblk * SEL_BLOCK <= tq
    impm = jnp.where(causal_blk, jnp.where(forced, BIG, imp), NEG_INF)
    rank = jnp.zeros((n_sel, TQ), F32)
    for k in range(n_sel):
        row = impm[k:k + 1, :]
        tie_k = jnp.where(jblk > k, 1.0, 0.0)
        rank = rank + jnp.where(row > impm, 1.0, 0.0) + jnp.where(row == impm, tie_k, 0.0)
    sel_t = jnp.where(rank < float(top_n), 1.0, 0.0)
    sel = jnp.concatenate([sel_t, jnp.zeros((LANES - n_sel, TQ), F32)], axis=0).T.astype(BF16)
    selk_ref[...] = _dot(sel, e_ref[...])
    q_halves = [q_lo, q_hi]
    c_hi = s0 // CK + 1

    def sel_bias(off):
        kpos = off + _lane_iota((1, CK_WIDE))
        ok = (selk_ref[:, pl.ds(off, CK_WIDE)] > 0.5) & (kpos <= t)
        return jnp.where(ok, 0.0, NEG_INF)

    _masked_softmax_pv(q_halves,
                       lambda off: kv_ref[0, pl.ds(off, CK_WIDE), 0:LANES],
                       lambda e, off: kv_ref[0, pl.ds(off, CK_WIDE), (1 + e) * LANES:(2 + e) * LANES],
                       sel_bias, 0, s0 // CK_WIDE + 1, s_ref, rmax_ref, acc_ref, ck=CK_WIDE)
    os2 = _norm_pair(acc_ref[0], acc_ref[1], lo)

    def win_bias(off):
        dist = t - (off + _lane_iota((1, CK)))
        return jnp.where((dist >= 0) & (dist < WINDOW), 0.0, NEG_INF)

    c_lo = jnp.maximum(s0 - (WINDOW - 1), 0) // CK
    _masked_softmax_pv(q_halves,
                       lambda off: kv_ref[0, pl.ds(off, CK), 3 * LANES:4 * LANES],
                       lambda e, off: kv_ref[0, pl.ds(off, CK), (4 + e) * LANES:(5 + e) * LANES],
                       win_bias, c_lo, c_hi, s_ref, rmax_ref, acc_ref)
    ow2 = _norm_pair(acc_ref[0], acc_ref[1], lo)

    gt = gate_ref[0]
    for p in range(4):
        rs = slice(p * TQ, (p + 1) * TQ)
        ca, cb = (2 * p) * 3, (2 * p + 1) * 3
        g_c = jnp.where(lo, gt[:, ca:ca + 1], gt[:, cb:cb + 1])
        g_s = jnp.where(lo, gt[:, ca + 1:ca + 2], gt[:, cb + 1:cb + 2])
        g_w = jnp.where(lo, gt[:, ca + 2:ca + 3], gt[:, cb + 2:cb + 3])
        oc2 = jnp.where(lo, oc_a[rs], oc_b[rs])
        o_ref[0, :, p * LANES:(p + 1) * LANES] = (g_c * oc2 + g_s * os2[rs] + g_w * ow2[rs]).astype(BF16)


def _nsa_attention(q, kv, cmpkv, gate, agg, emat, S):
    B = q.shape[0]
    ncp = cmpkv.shape[1]
    return pl.pallas_call(
        functools.partial(_nsa_attn_kernel, S=S),
        grid=(B, NSA_KV_GROUPS, S // TQ),
        in_specs=[
            pl.BlockSpec((1, TQ, 512), lambda b, g, i: (b, i, g)),
            pl.BlockSpec((1, S, 6 * LANES), lambda b, g, i: (b, 0, g)),
            pl.BlockSpec((1, ncp, 2 * LANES), lambda b, g, i: (b, 0, g)),
            pl.BlockSpec((1, TQ, LANES), lambda b, g, i: (b, i, g)),
            pl.BlockSpec((S // SEL_BLOCK, ncp), lambda b, g, i: (0, 0)),
            pl.BlockSpec((LANES, S), lambda b, g, i: (0, 0)),
        ],
        out_specs=pl.BlockSpec((1, TQ, 512), lambda b, g, i: (b, i, g)),
        out_shape=jax.ShapeDtypeStruct((B, S, 1024), BF16),
        scratch_shapes=[pltpu.VMEM((TQ, S), F32),
                        pltpu.VMEM((2, 4 * TQ, S), F32),
                        pltpu.VMEM((2, 4 * TQ, LANES), F32),
                        pltpu.VMEM((2, 4 * TQ, LANES), F32)],
        compiler_params=_cparams(("parallel", "parallel", "parallel")),
        name="nsa_attention",
    )(q, kv, cmpkv, gate, agg, emat)


def _dsa_in_kernel(x_ref, g_ref, w_ref, kvn_ref, wukv_ref, cos_ref, sin_ref,
                   q_ref, kv_ref, qi_ref, ki_ref, wi_ref):
    h = _rms(x_ref[...], g_ref[...]).astype(BF16)
    proj = _dot(h, w_ref[...])
    cos2 = cos_ref[...]
    sin2 = sin_ref[...]
    for m in range(8):
        slab = proj[:, m * LANES:(m + 1) * LANES]
        q_ref[:, m * LANES:(m + 1) * LANES] = (_rope128(slab, cos2, sin2) * (HEAD_DIM ** -0.5)).astype(BF16)
    c = _rms(proj[:, 1024:1280], kvn_ref[...]).astype(BF16)
    kvu = _dot(c, wukv_ref[...])
    kd, vd = _dup_halves(kvu)
    v_lo, v_hi = _ones_halves(vd)
    kv_ref[:, 0:LANES] = _rope128(kd, cos2, sin2).astype(BF16)
    kv_ref[:, LANES:2 * LANES] = v_lo.astype(BF16)
    kv_ref[:, 2 * LANES:3 * LANES] = v_hi.astype(BF16)
    for m in range(4):
        slab = proj[:, 1280 + m * LANES:1280 + (m + 1) * LANES]
        qi_ref[:, m * LANES:(m + 1) * LANES] = (_rope128(slab, cos2, sin2) * (IDX_DIM ** -0.5)).astype(BF16)
    last = proj[:, 1792:1920]
    kid, wid = _dup_halves(last)
    ki_ref[...] = _rope128(kid, cos2, sin2).astype(BF16)
    wi_ref[0] = (wid * (IDX_HEADS ** -0.5)).T[0:IDX_HEADS, :]


def _dsa_in_proj(x2d, g, w, kvn, wukv, cos2, sin2, S):
    T = x2d.shape[0]
    tm = 256
    tps = S // tm
    return pl.pallas_call(
        _dsa_in_kernel,
        grid=(T // tm,),
        in_specs=[
            pl.BlockSpec((tm, D_MODEL), lambda i: (i, 0)),
            pl.BlockSpec((1, D_MODEL), lambda i: (0, 0)),
            pl.BlockSpec((D_MODEL, IN_W_PAD), lambda i: (0, 0)),
            pl.BlockSpec((1, DSA_KV_RANK), lambda i: (0, 0)),
            pl.BlockSpec((DSA_KV_RANK, LANES), lambda i: (0, 0)),
            pl.BlockSpec((tm, LANES), lambda i: (i % tps, 0)),
            pl.BlockSpec((tm, LANES), lambda i: (i % tps, 0)),
        ],
        out_specs=[
            pl.BlockSpec((tm, 1024), lambda i: (i, 0)),
            pl.BlockSpec((tm, 3 * LANES), lambda i: (i, 0)),
            pl.BlockSpec((tm, 512), lambda i: (i, 0)),
            pl.BlockSpec((tm, LANES), lambda i: (i, 0)),
            pl.BlockSpec((1, IDX_HEADS, tm), lambda i: (i // tps, 0, i % tps)),
        ],
        out_shape=[
            jax.ShapeDtypeStruct((T, 1024), BF16),
            jax.ShapeDtypeStruct((T, 3 * LANES), BF16),
            jax.ShapeDtypeStruct((T, 512), BF16),
            jax.ShapeDtypeStruct((T, LANES), BF16),
            jax.ShapeDtypeStruct((T // S, IDX_HEADS, S), F32),
        ],
        compiler_params=_cparams(("parallel",)),
        name="dsa_in_proj",
    )(x2d, g, w, kvn, wukv, cos2, sin2)


def _sortable(v):
    bits = lax.bitcast_convert_type(v + 0.0, jnp.int32)
    return bits ^ ((bits >> 31) & jnp.int32(0x7FFFFFFF))


_NEG_BITS = np.array([NEG_INF], np.float32).view(np.int32)
_NEG_KEY = int((_NEG_BITS ^ ((_NEG_BITS >> 31) & np.int32(0x7FFFFFFF)))[0])
_INT_MIN = -(2 ** 31)


def _dsa_attn_kernel(q_ref, kv_ref, qi_ref, ki_ref, wi_ref, o_ref,
                     key_ref, bias_ref, s_ref, rmax_ref, acc_ref, *, S):
    top_k = min(DSA_TOP_K_MAX, S // 4)
    idx_bits = (S - 1).bit_length()
    i = pl.program_id(1)
    s0 = i * TQ
    n_chunks = s0 // CK + 1
    lo = _lane_iota((1, LANES)) < 64
    tq = s0 + _lane_iota((1, TQ))

    qi = qi_ref[0]
    qi2 = jnp.concatenate([qi[:, p * LANES:(p + 1) * LANES] for p in range(4)], axis=0)
    zi = jnp.zeros_like(qi2)
    qi_lo = jnp.where(lo, qi2, zi)
    qi_hi = jnp.where(lo, zi, qi2)
    wt = wi_ref[0]

    def idx_body(c, carry):
        off = pl.multiple_of(c * CK, CK)
        ki = ki_ref[0, pl.ds(off, CK), :]
        sa = _dot_nt(ki, qi_lo)
        sb = _dot_nt(ki, qi_hi)
        score = jnp.zeros((CK, TQ), F32)
        for p in range(4):
            cs = slice(p * TQ, (p + 1) * TQ)
            score = score + jnp.maximum(sa[:, cs], 0.0) * wt[2 * p:2 * p + 1, :]
            score = score + jnp.maximum(sb[:, cs], 0.0) * wt[2 * p + 1:2 * p + 2, :]
        kpos = off + lax.broadcasted_iota(jnp.int32, (CK, 1), 0)
        score = jnp.where(kpos <= tq, score, NEG_INF)
        key_ref[pl.ds(off, CK), :] = _sortable(score)
        return carry

    lax.fori_loop(0, n_chunks, idx_body, 0)

    n_out = (S - n_chunks * CK).astype(F32)

    def count(pred_fn):
        def body(c, acc):
            off = pl.multiple_of(c * CK, CK)
            hit = jnp.where(pred_fn(key_ref[pl.ds(off, CK), :], off), 1.0, 0.0)
            return acc + _tree_sum([hit[8 * j:8 * j + 8] for j in range(CK // 8)])
        acc = lax.fori_loop(0, n_chunks, body, jnp.zeros((8, TQ), F32))
        return jnp.sum(acc, axis=0, keepdims=True)

    def bit_body(it, tu):
        bit = jnp.left_shift(jnp.int32(1), 31 - it)
        cand = (tu | bit) ^ jnp.int32(_INT_MIN)
        cnt = count(lambda kk, off: kk >= cand)
        cnt = cnt + n_out * jnp.where(jnp.int32(_NEG_KEY) >= cand, 1.0, 0.0)
        return jnp.where(cnt >= float(top_k), tu | bit, tu)

    tu = lax.fori_loop(0, 32, bit_body, jnp.zeros((1, TQ), jnp.int32))
    thr = tu ^ jnp.int32(_INT_MIN)
    n_gt = count(lambda kk, off: kk > thr)
    n_gt = n_gt + n_out * jnp.where(jnp.int32(_NEG_KEY) > thr, 1.0, 0.0)
    need = float(top_k) - n_gt
    n_eq = count(lambda kk, off: kk == thr)

    def row_iota(off):
        return off + lax.broadcasted_iota(jnp.int32, (CK, 1), 0)

    def tie_search():
        def tie_body(it, jb):
            bit = jnp.left_shift(jnp.int32(1), idx_bits - 1 - it)
            cand = jb | bit
            cnt = count(lambda kk, off: (kk == thr) & (row_iota(off) < cand))
            return jnp.where(cnt < need, cand, jb)
        return lax.fori_loop(0, idx_bits, tie_body, jnp.zeros((1, TQ), jnp.int32))

    jb = lax.cond(jnp.max(n_eq - need) > 0.0, tie_search,
                  lambda: jnp.full((1, TQ), 2 ** idx_bits - 1, jnp.int32))

    eye = jnp.where(lax.broadcasted_iota(jnp.int32, (TQ, TQ), 0) == _lane_iota((TQ, TQ)), 1.0, 0.0).astype(BF16)

    def bias_body(c, carry):
        off = pl.multiple_of(c * CK, CK)
        kk = key_ref[pl.ds(off, CK), :]
        kpos = row_iota(off)
        picked = ((kk > thr) | ((kk == thr) & (kpos <= jb))) & (kpos <= tq)
        sel_qk = _dot_nt(eye, jnp.where(picked, 1.0, 0.0).astype(BF16))
        bias_ref[:, pl.ds(off, CK)] = (sel_qk - 1.0) * BIG
        return carry

    lax.fori_loop(0, n_chunks, bias_body, 0)

    @pl.when(n_chunks * CK < (s0 // CK_WIDE + 1) * CK_WIDE)
    def _():
        bias_ref[:, pl.ds(pl.multiple_of(n_chunks * CK, CK), CK)] = jnp.full((TQ, CK), NEG_INF, F32)

    q = q_ref[0]
    q2 = jnp.concatenate([q[:, p * LANES:(p + 1) * LANES] for p in range(8)], axis=0)
    zq = jnp.zeros_like(q2)
    for e, qh in enumerate((jnp.where(lo, q2, zq), jnp.where(lo, zq, q2))):
        _masked_softmax_pv([qh],
                           lambda off: kv_ref[0, pl.ds(off, CK_WIDE), 0:LANES],
                           lambda _, off, e=e: kv_ref[0, pl.ds(off, CK_WIDE), (1 + e) * LANES:(2 + e) * LANES],
                           lambda off: bias_ref[:, pl.ds(off, CK_WIDE)],
                           0, s0 // CK_WIDE + 1, s_ref, rmax_ref, acc_ref.at[pl.ds(e, 1)], ck=CK_WIDE)
    o2 = _norm_pair(acc_ref[0], acc_ref[1], lo)
    for p in range(8):
        o_ref[0, :, p * LANES:(p + 1) * LANES] = o2[p * TQ:(p + 1) * TQ].astype(BF16)


def _dsa_attention(q, kv, qi, ki, wi, S):
    B = q.shape[0]
    return pl.pallas_call(
        functools.partial(_dsa_attn_kernel, S=S),
        grid=(B, S // TQ),
        in_specs=[
            pl.BlockSpec((1, TQ, 1024), lambda b, i: (b, i, 0)),
            pl.BlockSpec((1, S, 3 * LANES), lambda b, i: (b, 0, 0)),
            pl.BlockSpec((1, TQ, 512), lambda b, i: (b, i, 0)),
            pl.BlockSpec((1, S, LANES), lambda b, i: (b, 0, 0)),
            pl.BlockSpec((1, IDX_HEADS, TQ), lambda b, i: (b, 0, i)),
        ],
        out_specs=pl.BlockSpec((1, TQ, 1024), lambda b, i: (b, i, 0)),
        out_shape=jax.ShapeDtypeStruct((B, S, 1024), BF16),
        scratch_shapes=[pltpu.VMEM((S, TQ), jnp.int32),
                        pltpu.VMEM((TQ, S), F32),
                        pltpu.VMEM((1, 8 * TQ, S), F32),
                        pltpu.VMEM((1, 8 * TQ, LANES), F32),
                        pltpu.VMEM((2, 8 * TQ, LANES), F32)],
        compiler_params=_cparams(("parallel", "parallel")),
        name="dsa_attention",
    )(q, kv, qi, ki, wi)


def _out_proj_kernel(a_ref, w_ref, x_ref, g_ref, o_ref):
    y = _dot(a_ref[...], w_ref[...])
    o_ref[...] = x_ref[...] + _rms(y, g_ref[...])


def _out_proj(a2d, w, x2d, g):
    T = x2d.shape[0]
    tm = 512
    return pl.pallas_call(
        _out_proj_kernel,
        grid=(T // tm,),
        in_specs=[
            pl.BlockSpec((tm, 1024), lambda i: (i, 0)),
            pl.BlockSpec((1024, D_MODEL), lambda i: (0, 0)),
            pl.BlockSpec((tm, D_MODEL), lambda i: (i, 0)),
            pl.BlockSpec((1, D_MODEL), lambda i: (0, 0)),
        ],
        out_specs=pl.BlockSpec((tm, D_MODEL), lambda i: (i, 0)),
        out_shape=jax.ShapeDtypeStruct((T, D_MODEL), F32),
        compiler_params=_cparams(("parallel",)),
        name="mixer_out_proj",
    )(a2d, w, x2d, g)


FFN_TM = 512
FFN_FC = 256
FFN_NF = D_FF // FFN_FC
FFN_HALO = 16


def _ffn_kernel(x_ref, xh_ref, g2_ref, g3_ref, wup_ref, cw_ref, cb_ref, wd_ref, o_ref, hext, ubuf,
                *, tiles_per_seq):
    i = pl.program_id(0)
    tm = FFN_TM
    g2 = g2_ref[...]
    x = x_ref[...]
    hext[FFN_HALO:, :] = _rms(x, g2).astype(BF16)
    hh = _rms(xh_ref[...], g2)
    hh = jnp.where(i % tiles_per_seq == 0, 0.0, hh)
    hext[0:FFN_HALO, :] = hh.astype(BF16)
    he = hext[...]

    def conv(u_ref, col):
        w = cw_ref[:, col:col + FFN_FC]
        return (w[2:3] * u_ref[FFN_HALO:FFN_HALO + tm, :]
                + w[1:2] * u_ref[FFN_HALO - 1:FFN_HALO - 1 + tm, :]
                + w[0:1] * u_ref[FFN_HALO - 2:FFN_HALO - 2 + tm, :]
                + cb_ref[:, col:col + FFN_FC])

    y = jnp.zeros((tm, D_MODEL), F32)
    for f in range(FFN_NF):
        cv = f * FFN_FC
        cg = D_FF + f * FFN_FC
        uv = ubuf.at[(2 * f) % 4]
        ug = ubuf.at[(2 * f + 1) % 4]
        uv[...] = _dot(he, wup_ref[:, cv:cv + FFN_FC])
        ug[...] = _dot(he, wup_ref[:, cg:cg + FFN_FC])
        act = (_gelu_tanh(conv(ug, cg)) * conv(uv, cv)).astype(BF16)
        y = y + _dot(act, wd_ref[cv:cv + FFN_FC, :])
    o_ref[...] = x + _rms(y, g3_ref[...])


def _ffn(x2d, g2, g3, wup, cw, cb, wd, S):
    T = x2d.shape[0]
    tm = FFN_TM
    hb = tm // FFN_HALO
    const = lambda i: (0, 0)
    return pl.pallas_call(
        functools.partial(_ffn_kernel, tiles_per_seq=S // tm),
        grid=(T // tm,),
        in_specs=[
            pl.BlockSpec((tm, D_MODEL), lambda i: (i, 0)),
            pl.BlockSpec((FFN_HALO, D_MODEL), lambda i: (jnp.maximum(i * hb - 1, 0), 0)),
            pl.BlockSpec((1, D_MODEL), const),
            pl.BlockSpec((1, D_MODEL), const),
            pl.BlockSpec((D_MODEL, 2 * D_FF), const),
            pl.BlockSpec((CONV_WIDTH, 2 * D_FF), const),
            pl.BlockSpec((1, 2 * D_FF), const),
            pl.BlockSpec((D_FF, D_MODEL), const),
        ],
        out_specs=pl.BlockSpec((tm, D_MODEL), lambda i: (i, 0)),
        out_shape=jax.ShapeDtypeStruct((T, D_MODEL), F32),
        scratch_shapes=[
            pltpu.VMEM((tm + FFN_HALO, D_MODEL), BF16),
            pltpu.VMEM((4, tm + FFN_HALO, FFN_FC), F32),
        ],
        compiler_params=_cparams(("parallel",)),
        name="conv_ffn",
    )(x2d, x2d, g2, g3, wup, cw, cb, wd)


def _rope_tables(pos):
    half = HEAD_DIM // 2
    inv = ROPE_THETA ** (-jnp.arange(half, dtype=F32) / half)
    ang = pos.astype(F32)[:, None] * inv[None, :]
    cos, sin = jnp.cos(ang), jnp.sin(ang)
    return jnp.tile(cos, (1, 4)), jnp.tile(jnp.concatenate([-sin, sin], axis=-1), (1, 2))


def _tables(S):
    ncp = S // CMP_STRIDE
    n_sel = S // SEL_BLOCK
    cos2, sin2 = _rope_tables(jnp.arange(S, dtype=jnp.int32))
    cosk, sink = _rope_tables(jnp.arange(ncp, dtype=jnp.int32) * CMP_STRIDE + (CMP_BLOCK - 1))
    cosc = jnp.stack([cosk, jnp.ones_like(cosk)])
    sinc = jnp.stack([sink, jnp.zeros_like(sink)])
    sj = np.arange(n_sel)[:, None]
    ci = np.arange(ncp)[None, :]
    agg_t = ((ci * CMP_STRIDE <= sj * SEL_BLOCK + SEL_BLOCK - 1)
             & (ci * CMP_STRIDE + CMP_BLOCK - 1 >= sj * SEL_BLOCK)
             & (ci < (S - CMP_BLOCK) // CMP_STRIDE + 1))
    agg_t = jnp.asarray(agg_t, dtype=BF16)
    emat = jnp.asarray(np.arange(LANES)[:, None] == (np.arange(S)[None, :] // SEL_BLOCK), dtype=BF16)
    return (cos2, sin2, cosc, sinc, agg_t, emat)


def _pad_cols(w, width):
    return jnp.pad(w, ((0, 0), (0, width - w.shape[1])))


def _nsa_mixer(x2d, g_pre, g_post, w_in, pe_k, w1_k, w2_k, pe_v, w1_v, w2_v, w_out, B, S, tabs):
    cos2, sin2, cosc, sinc, agg, emat = tabs
    ncp = S // CMP_STRIDE
    q, kv, cv, gate = _nsa_in_proj(x2d, g_pre, _pad_cols(w_in, IN_W_PAD).astype(BF16), cos2, sin2, S)
    chunks = cv.reshape(B, ncp, CMP_STRIDE, 2, NSA_KV_GROUPS, HEAD_DIM).transpose(0, 3, 4, 1, 2, 5)
    chunks = chunks.reshape(B, 4, ncp, CMP_STRIDE * HEAD_DIM)
    nxt = jnp.concatenate([chunks[:, :, 1:], chunks[:, :, :1]], axis=2)
    blocks = jnp.concatenate([chunks, nxt], axis=-1)
    pe = jnp.stack([pe_k.reshape(1, -1), pe_v.reshape(1, -1)])
    w1 = jnp.stack([w1_k, w1_v]).astype(BF16)
    w2d = jnp.stack([jnp.tile(w2_k, (1, 2)), jnp.tile(w2_v, (1, 2))]).astype(BF16)
    cmp4 = _nsa_compress(blocks, pe, w1, w2d, cosc, sinc)
    cmpkv = jnp.concatenate([cmp4[:, :, 0:128], cmp4[:, :, 256:384], cmp4[:, :, 128:256], cmp4[:, :, 384:512]], axis=-1)
    o = _nsa_attention(q.reshape(B, S, 1024), kv.reshape(B, S, 12 * LANES), cmpkv,
                       gate.reshape(B, S, 256), agg, emat, S)
    return _out_proj(o.reshape(B * S, 1024), w_out.astype(BF16), x2d, g_post)


def _dsa_mixer(x2d, g_pre, g_post, w_in, kv_norm, w_uk, w_uv, w_out, B, S, tabs):
    cos2, sin2 = tabs[0], tabs[1]
    wukv = jnp.concatenate([w_uk, w_uv], axis=1).astype(BF16)
    q, kv, qi, ki, wi = _dsa_in_proj(x2d, g_pre, _pad_cols(w_in, IN_W_PAD).astype(BF16),
                                     kv_norm.reshape(1, -1), wukv, cos2, sin2, S)
    o = _dsa_attention(q.reshape(B, S, 1024), kv.reshape(B, S, 3 * LANES), qi.reshape(B, S, 512),
                       ki.reshape(B, S, LANES), wi, S)
    return _out_proj(o.reshape(B * S, 1024), w_out.astype(BF16), x2d, g_post)


def kernel(x, norm_g, nsa_w_in, nsa_cmp_pe_k, nsa_cmp_w1_k, nsa_cmp_w2_k, nsa_cmp_pe_v, nsa_cmp_w1_v, nsa_cmp_w2_v, nsa_w_out, dsa_w_in, dsa_kv_norm, dsa_w_uk, dsa_w_uv, dsa_w_out, ffn_w_up, ffn_conv_w, ffn_conv_b, ffn_w_down):
    B, S, _ = x.shape
    tabs = _tables(S)
    x2d = x.reshape(B * S, D_MODEL)
    for i in range(DEPTH):
        g = norm_g[i].reshape(4, 1, D_MODEL)
        j = i // 2
        if i % 2 == 0:
            x2d = _nsa_mixer(x2d, g[0], g[1], nsa_w_in[j], nsa_cmp_pe_k[j], nsa_cmp_w1_k[j], nsa_cmp_w2_k[j],
                             nsa_cmp_pe_v[j], nsa_cmp_w1_v[j], nsa_cmp_w2_v[j], nsa_w_out[j], B, S, tabs)
        else:
            x2d = _dsa_mixer(x2d, g[0], g[1], dsa_w_in[j], dsa_kv_norm[j], dsa_w_uk[j], dsa_w_uv[j],
                             dsa_w_out[j], B, S, tabs)
        x2d = _ffn(x2d, g[2], g[3], ffn_w_up[i].astype(BF16), ffn_conv_w[i],
                   ffn_conv_b[i].reshape(1, -1), ffn_w_down[i].astype(BF16), S)
    return x2d.reshape(B, S, D_MODEL)
```

```python
import functools

import numpy as np
import jax
import jax.numpy as jnp
from jax import lax
from jax.experimental import pallas as pl
from jax.experimental.pallas import tpu as pltpu

F32 = jnp.float32
BF16 = jnp.bfloat16

D_MODEL = 1024
DEPTH = 4
N_HEADS = 16
HEAD_DIM = 64
ROPE_THETA = 10000.0
RMS_EPS = 1e-6
NEG_INF = -1e30
BIG = 1e30

NSA_KV_GROUPS = 2
CMP_BLOCK = 32
CMP_STRIDE = 16
CMP_HIDDEN = 256
SEL_BLOCK = 64
SEL_TOP_N = 16
WINDOW = 512

DSA_KV_RANK = 256
IDX_HEADS = 8
IDX_DIM = 64
DSA_TOP_K_MAX = 256

D_FF = 2816
CONV_WIDTH = 3

NSA_IN_W = N_HEADS * HEAD_DIM + 6 * NSA_KV_GROUPS * HEAD_DIM + 3 * N_HEADS
DSA_IN_W = N_HEADS * HEAD_DIM + DSA_KV_RANK + IDX_HEADS * IDX_DIM + IDX_DIM + IDX_HEADS

Q_SCALE = HEAD_DIM ** -0.5 * 1.4426950408889634

LANES = 128
IN_W_PAD = 1920
TQ = 256
CK = 256
CK_WIDE = 512
VMEM_LIMIT = 56 * 1024 * 1024


def _cparams(sem):
    return pltpu.CompilerParams(dimension_semantics=sem, vmem_limit_bytes=VMEM_LIMIT)


def _rms(x, g):
    return x * lax.rsqrt(jnp.mean(x * x, axis=-1, keepdims=True) + RMS_EPS) * g


def _lane_iota(shape):
    return lax.broadcasted_iota(jnp.int32, shape, len(shape) - 1)


def _rope128(x, cos2, sin2):
    lane = _lane_iota(x.shape)
    first = (lane % 64) < 32
    swapped = jnp.where(first, pltpu.roll(x, 96, axis=1), pltpu.roll(x, 32, axis=1))
    return x * cos2 + swapped * sin2


def _dup_halves(x):
    lo = _lane_iota(x.shape) < 64
    r = pltpu.roll(x, 64, axis=1)
    return jnp.where(lo, x, r), jnp.where(lo, r, x)


def _dot(a, b):
    return jnp.dot(a, b, preferred_element_type=F32)


def _dot_nt(a, b):
    return lax.dot_general(a, b, (((1,), (1,)), ((), ())), preferred_element_type=F32)


def _tree_sum(parts):
    while len(parts) > 1:
        parts = [parts[j] + parts[j + 1] if j + 1 < len(parts) else parts[j] for j in range(0, len(parts), 2)]
    return parts[0]


def _gelu_tanh(x):
    return 0.5 * x * (1.0 + jnp.tanh(0.7978845608028654 * (x + 0.044715 * (x * x * x))))


def _ones_halves(x):
    lo = _lane_iota(x.shape) < 64
    return jnp.where(lo, x, 1.0), jnp.where(lo, 1.0, x)


def _masked_softmax_pv(q_halves, k_of, v_of, bias_of, c_lo, c_hi, s_ref, rmax_ref, acc_ref, ck=CK):
    n_e = len(q_halves)
    rows = q_halves[0].shape[0]
    r = rows // TQ
    for e in range(n_e):
        rmax_ref[e] = jnp.full((rows, LANES), NEG_INF, F32)
        acc_ref[e] = jnp.zeros((rows, LANES), F32)

    def pass1(c, carry):
        off = pl.multiple_of(c * ck, ck)
        k = k_of(off)
        bias = bias_of(off)
        for e in range(n_e):
            for j in range(r):
                rs = slice(j * TQ, (j + 1) * TQ)
                s = _dot_nt(q_halves[e][rs], k) + bias
                s_ref[e, rs, pl.ds(off, ck)] = s
                cols = [s[:, i * LANES:(i + 1) * LANES] for i in range(ck // LANES)]
                rmax_ref[e, rs] = functools.reduce(jnp.maximum, cols, rmax_ref[e, rs])
        return carry

    lax.fori_loop(c_lo, c_hi, pass1, 0)
    for e in range(n_e):
        m = jnp.max(rmax_ref[e], axis=-1, keepdims=True)
        m = jnp.where(m > NEG_INF, m, 0.0)
        rmax_ref[e] = jnp.broadcast_to(m, (rows, LANES))

    def pass2(c, carry):
        off = pl.multiple_of(c * ck, ck)
        for e in range(n_e):
            v = v_of(e, off)
            for j in range(r):
                rs = slice(j * TQ, (j + 1) * TQ)
                mb = rmax_ref[e, rs]
                p = jnp.exp2(s_ref[e, rs, pl.ds(off, ck)] - jnp.concatenate([mb] * (ck // LANES), axis=1))
                acc_ref[e, rs] += _dot(p.astype(BF16), v)
        return carry

    lax.fori_loop(c_lo, c_hi, pass2, 0)


def _norm_pair(acc_a, acc_b, lo):
    num = jnp.where(lo, acc_a, acc_b)
    den = pltpu.roll(jnp.where(lo, acc_b, acc_a), 64, axis=1)
    return jnp.where(den > 0.0, num / jnp.where(den > 0.0, den, 1.0), 0.0)


def _nsa_in_kernel(x_ref, g_ref, w_ref, cos_ref, sin_ref, q_ref, kv_ref, cv_ref, gate_ref):
    h = _rms(x_ref[...], g_ref[...]).astype(BF16)
    proj = _dot(h, w_ref[...])
    cos2 = cos_ref[...]
    sin2 = sin_ref[...]
    for m in range(8):
        slab = proj[:, m * LANES:(m + 1) * LANES]
        q_ref[:, m * LANES:(m + 1) * LANES] = (_rope128(slab, cos2, sin2) * Q_SCALE).astype(BF16)
    cv_ref[...] = proj[:, 1024:1280]
    ks = _rope128(proj[:, 1280:1408], cos2, sin2)
    vs = proj[:, 1408:1536]
    kw = _rope128(proj[:, 1536:1664], cos2, sin2)
    vw = proj[:, 1664:1792]
    for j, (kk, vv) in enumerate(((ks, vs), (kw, vw))):
        k0, k1 = _dup_halves(kk)
        v0, v1 = _dup_halves(vv)
        for g, (kd, vd) in enumerate(((k0, v0), (k1, v1))):
            base = (g * 6 + j * 3) * LANES
            v_lo, v_hi = _ones_halves(vd)
            kv_ref[:, base:base + LANES] = kd.astype(BF16)
            kv_ref[:, base + LANES:base + 2 * LANES] = v_lo.astype(BF16)
            kv_ref[:, base + 2 * LANES:base + 3 * LANES] = v_hi.astype(BF16)
    gt = proj[:, 1792:1920]
    sg = 1.0 / (1.0 + jnp.exp(-gt))
    gate_ref[:, 0:LANES] = sg
    gate_ref[:, LANES:2 * LANES] = pltpu.roll(sg, LANES - 24, axis=1)


def _nsa_in_proj(x2d, g, w, cos2, sin2, S):
    T = x2d.shape[0]
    tm = 256
    tps = S // tm
    return pl.pallas_call(
        _nsa_in_kernel,
        grid=(T // tm,),
        in_specs=[
            pl.BlockSpec((tm, D_MODEL), lambda i: (i, 0)),
            pl.BlockSpec((1, D_MODEL), lambda i: (0, 0)),
            pl.BlockSpec((D_MODEL, IN_W_PAD), lambda i: (0, 0)),
            pl.BlockSpec((tm, LANES), lambda i: (i % tps, 0)),
            pl.BlockSpec((tm, LANES), lambda i: (i % tps, 0)),
        ],
        out_specs=[
            pl.BlockSpec((tm, 1024), lambda i: (i, 0)),
            pl.BlockSpec((tm, 12 * LANES), lambda i: (i, 0)),
            pl.BlockSpec((tm, 256), lambda i: (i, 0)),
            pl.BlockSpec((tm, 256), lambda i: (i, 0)),
        ],
        out_shape=[
            jax.ShapeDtypeStruct((T, 1024), BF16),
            jax.ShapeDtypeStruct((T, 12 * LANES), BF16),
            jax.ShapeDtypeStruct((T, 256), F32),
            jax.ShapeDtypeStruct((T, 256), F32),
        ],
        compiler_params=_cparams(("parallel",)),
        name="nsa_in_proj",
    )(x2d, g, w, cos2, sin2)


def _nsa_cmp_kernel(blk_ref, pe_ref, w1_ref, w2_ref, cos_ref, sin_ref, o_ref):
    xb = (blk_ref[0, 0] + pe_ref[0]).astype(BF16)
    hid = _gelu_tanh(_dot(xb, w1_ref[0]))
    out = _dot(hid.astype(BF16), w2_ref[0])
    o_ref[0] = _rope128(out, cos_ref[0], sin_ref[0]).astype(BF16)


def _nsa_compress(blocks, pe, w1, w2d, cosc, sinc):
    B, _, ncp, width = blocks.shape
    return pl.pallas_call(
        _nsa_cmp_kernel,
        grid=(B, 4),
        in_specs=[
            pl.BlockSpec((1, 1, ncp, width), lambda b, j: (b, j, 0, 0)),
            pl.BlockSpec((1, 1, width), lambda b, j: (j // 2, 0, 0)),
            pl.BlockSpec((1, width, CMP_HIDDEN), lambda b, j: (j // 2, 0, 0)),
            pl.BlockSpec((1, CMP_HIDDEN, LANES), lambda b, j: (j // 2, 0, 0)),
            pl.BlockSpec((1, ncp, LANES), lambda b, j: (j // 2, 0, 0)),
            pl.BlockSpec((1, ncp, LANES), lambda b, j: (j // 2, 0, 0)),
        ],
        out_specs=pl.BlockSpec((1, ncp, LANES), lambda b, j: (b, 0, j)),
        out_shape=jax.ShapeDtypeStruct((B, ncp, 4 * LANES), BF16),
        compiler_params=_cparams(("parallel", "parallel")),
        name="nsa_compress",
    )(blocks, pe, w1, w2d, cosc, sinc)


def _nsa_attn_kernel(q_ref, kv_ref, cmp_ref, gate_ref, agg_ref, e_ref, o_ref,
                     s_ref, rmax_ref, acc_ref, *, S):
    n_sel = S // SEL_BLOCK
    top_n = min(SEL_TOP_N, n_sel)
    i = pl.program_id(2)
    s0 = i * TQ
    lo = _lane_iota((1, LANES)) < 64
    q = q_ref[0]
    q2 = jnp.concatenate([q[:, p * LANES:(p + 1) * LANES] for p in range(4)], axis=0)
    zero = jnp.zeros_like(q2)
    q_lo = jnp.where(lo, q2, zero)
    q_hi = jnp.where(lo, zero, q2)
    t = s0 + lax.broadcasted_iota(jnp.int32, (TQ, 1), 0)

    kc = cmp_ref[0, :, 0:LANES]
    ncp = kc.shape[0]
    cend = _lane_iota((1, ncp)) * CMP_STRIDE + (CMP_BLOCK - 1)
    mask_c = (cend <= t)[None]

    def cmp_branch(qh, vc1):
        s3 = jnp.where(mask_c, _dot_nt(qh, kc).reshape(4, TQ, ncp), NEG_INF)
        e = jnp.where(mask_c, jnp.exp2(s3 - jnp.max(s3, axis=-1, keepdims=True)), 0.0)
        e = e.reshape(4 * TQ, ncp)
        return e, _dot(e.astype(BF16), vc1)

    ea, acc_ca = cmp_branch(q_lo, cmp_ref[0, :, LANES:2 * LANES])
    eb, acc_cb = cmp_branch(q_hi, cmp_ref[0, :, 2 * LANES:3 * LANES])
    oc2 = _norm_pair(acc_ca, acc_cb, lo)
    den = pltpu.roll(jnp.where(lo, acc_cb, acc_ca), 64, axis=1)
    l_a = jnp.where(lo, den, acc_ca)
    l_b = jnp.where(lo, acc_cb, den)
    pa = ea / jnp.where(l_a > 0.0, l_a, 1.0)
    pb = eb / jnp.where(l_b > 0.0, l_b, 1.0)
    psum = _tree_sum([pa[p * TQ:(p + 1) * TQ] for p in range(4)] + [pb[p * TQ:(p + 1) * TQ] for p in range(4)])
    agg = agg_ref[...]
    p_hi = psum.astype(BF16)
    r1 = psum - p_hi.astype(F32)
    p_mid = r1.astype(BF16)
    p_lo = (r1 - p_mid.astype(F32)).astype(BF16)
    imp = _dot_nt(agg, p_hi) + _dot_nt(agg, p_mid) + _dot_nt(agg, p_lo)

    tq = s0 + _lane_iota((1, TQ))
    jblk = lax.broadcasted_iota(jnp.int32, (n_sel, 1), 0)
    cur = jnp.right_shift(tq, SEL_BLOCK.bit_length() - 1)
    forced = (jblk == 0) | (jblk == cur) | (jblk == cur - 1)
    causal_blk = jblk * SEL_BLOCK <= tq
    impm = jnp.where(causal_blk, jnp.where(forced, BIG, imp), NEG_INF)
    rank = jnp.zeros((n_sel, TQ), F32)
    for k in range(n_sel):
        row = impm[k:k + 1, :]
        tie_k = jnp.where(jblk > k, 1.0, 0.0)
        rank = rank + jnp.where(row > impm, 1.0, 0.0) + jnp.where(row == impm, tie_k, 0.0)
    sel_t = jnp.where(rank < float(top_n), 1.0, 0.0)
    sel = jnp.concatenate([sel_t, jnp.zeros((LANES - n_sel, TQ), F32)], axis=0).T.astype(BF16)
    q_halves = [q_lo, q_hi]
    c_hi = s0 // CK + 1

    def sel_bias(off):
        kpos = off + _lane_iota((1, CK_WIDE))
        picked = _dot(sel, e_ref[:, pl.ds(off, CK_WIDE)])
        return jnp.where(picked > 0.5, jnp.where(kpos <= t, 0.0, NEG_INF), NEG_INF)

    _masked_softmax_pv(q_halves,
                       lambda off: kv_ref[0, pl.ds(off, CK_WIDE), 0:LANES],
                       lambda e, off: kv_ref[0, pl.ds(off, CK_WIDE), (1 + e) * LANES:(2 + e) * LANES],
                       sel_bias, 0, s0 // CK_WIDE + 1, s_ref, rmax_ref, acc_ref, ck=CK_WIDE)
    os2 = _norm_pair(acc_ref[0], acc_ref[1], lo)

    def win_bias(off):
        dist = t - (off + _lane_iota((1, CK)))
        return jnp.where(dist >= 0, jnp.where(dist < WINDOW, 0.0, NEG_INF), NEG_INF)

    c_lo = jnp.maximum(s0 - (WINDOW - 1), 0) // CK
    _masked_softmax_pv(q_halves,
                       lambda off: kv_ref[0, pl.ds(off, CK), 3 * LANES:4 * LANES],
                       lambda e, off: kv_ref[0, pl.ds(off, CK), (4 + e) * LANES:(5 + e) * LANES],
                       win_bias, c_lo, c_hi, s_ref, rmax_ref, acc_ref)
    ow2 = _norm_pair(acc_ref[0], acc_ref[1], lo)

    gt = gate_ref[0]
    for p in range(4):
        rs = slice(p * TQ, (p + 1) * TQ)
        ca, cb = (2 * p) * 3, (2 * p + 1) * 3
        g_c = jnp.where(lo, gt[:, ca:ca + 1], gt[:, cb:cb + 1])
        g_s = jnp.where(lo, gt[:, ca + 1:ca + 2], gt[:, cb + 1:cb + 2])
        g_w = jnp.where(lo, gt[:, ca + 2:ca + 3], gt[:, cb + 2:cb + 3])
        o_ref[0, :, p * LANES:(p + 1) * LANES] = (g_c * oc2[rs] + g_s * os2[rs] + g_w * ow2[rs]).astype(BF16)


def _nsa_attention(q, kv, cmpkv, gate, agg, emat, S):
    B = q.shape[0]
    ncp = cmpkv.shape[1]
    return pl.pallas_call(
        functools.partial(_nsa_attn_kernel, S=S),
        grid=(B, NSA_KV_GROUPS, S // TQ),
        in_specs=[
            pl.BlockSpec((1, TQ, 512), lambda b, g, i: (b, i, g)),
            pl.BlockSpec((1, S, 6 * LANES), lambda b, g, i: (b, 0, g)),
            pl.BlockSpec((1, ncp, 3 * LANES), lambda b, g, i: (b, 0, g)),
            pl.BlockSpec((1, TQ, LANES), lambda b, g, i: (b, i, g)),
            pl.BlockSpec((S // SEL_BLOCK, ncp), lambda b, g, i: (0, 0)),
            pl.BlockSpec((LANES, S), lambda b, g, i: (0, 0)),
        ],
        out_specs=pl.BlockSpec((1, TQ, 512), lambda b, g, i: (b, i, g)),
        out_shape=jax.ShapeDtypeStruct((B, S, 1024), BF16),
        scratch_shapes=[pltpu.VMEM((2, 4 * TQ, S), F32),
                        pltpu.VMEM((2, 4 * TQ, LANES), F32),
                        pltpu.VMEM((2, 4 * TQ, LANES), F32)],
        compiler_params=_cparams(("parallel", "parallel", "parallel")),
        name="nsa_attention",
    )(q, kv, cmpkv, gate, agg, emat)


def _dsa_in_kernel(x_ref, g_ref, w_ref, kvn_ref, wukv_ref, cos_ref, sin_ref,
                   q_ref, kv_ref, qi_ref, ki_ref, wi_ref):
    h = _rms(x_ref[...], g_ref[...]).astype(BF16)
    proj = _dot(h, w_ref[...])
    cos2 = cos_ref[...]
    sin2 = sin_ref[...]
    for m in range(8):
        slab = proj[:, m * LANES:(m + 1) * LANES]
        q_ref[:, m * LANES:(m + 1) * LANES] = (_rope128(slab, cos2, sin2) * Q_SCALE).astype(BF16)
    c = _rms(proj[:, 1024:1280], kvn_ref[...]).astype(BF16)
    kvu = _dot(c, wukv_ref[...])
    kd, vd = _dup_halves(kvu)
    v_lo, v_hi = _ones_halves(vd)
    kv_ref[:, 0:LANES] = _rope128(kd, cos2, sin2).astype(BF16)
    kv_ref[:, LANES:2 * LANES] = v_lo.astype(BF16)
    kv_ref[:, 2 * LANES:3 * LANES] = v_hi.astype(BF16)
    for m in range(4):
        slab = proj[:, 1280 + m * LANES:1280 + (m + 1) * LANES]
        qi_ref[:, m * LANES:(m + 1) * LANES] = (_rope128(slab, cos2, sin2) * (IDX_DIM ** -0.5)).astype(BF16)
    last = proj[:, 1792:1920]
    kid, wid = _dup_halves(last)
    ki_ref[...] = _rope128(kid, cos2, sin2).astype(BF16)
    wi_ref[0] = (wid * (IDX_HEADS ** -0.5)).T[0:IDX_HEADS, :]


def _dsa_in_proj(x2d, g, w, kvn, wukv, cos2, sin2, S):
    T = x2d.shape[0]
    tm = 256
    tps = S // tm
    return pl.pallas_call(
        _dsa_in_kernel,
        grid=(T // tm,),
        in_specs=[
            pl.BlockSpec((tm, D_MODEL), lambda i: (i, 0)),
            pl.BlockSpec((1, D_MODEL), lambda i: (0, 0)),
            pl.BlockSpec((D_MODEL, IN_W_PAD), lambda i: (0, 0)),
            pl.BlockSpec((1, DSA_KV_RANK), lambda i: (0, 0)),
            pl.BlockSpec((DSA_KV_RANK, LANES), lambda i: (0, 0)),
            pl.BlockSpec((tm, LANES), lambda i: (i % tps, 0)),
            pl.BlockSpec((tm, LANES), lambda i: (i % tps, 0)),
        ],
        out_specs=[
            pl.BlockSpec((tm, 1024), lambda i: (i, 0)),
            pl.BlockSpec((tm, 3 * LANES), lambda i: (i, 0)),
            pl.BlockSpec((tm, 512), lambda i: (i, 0)),
            pl.BlockSpec((tm, LANES), lambda i: (i, 0)),
            pl.BlockSpec((1, IDX_HEADS, tm), lambda i: (i // tps, 0, i % tps)),
        ],
        out_shape=[
            jax.ShapeDtypeStruct((T, 1024), BF16),
            jax.ShapeDtypeStruct((T, 3 * LANES), BF16),
            jax.ShapeDtypeStruct((T, 512), BF16),
            jax.ShapeDtypeStruct((T, LANES), BF16),
            jax.ShapeDtypeStruct((T // S, IDX_HEADS, S), F32),
        ],
        compiler_params=_cparams(("parallel",)),
        name="dsa_in_proj",
    )(x2d, g, w, kvn, wukv, cos2, sin2)


def _sortable(v):
    bits = lax.bitcast_convert_type(v + 0.0, jnp.int32)
    return bits ^ ((bits >> 31) & jnp.int32(0x7FFFFFFF))


_NEG_BITS = np.array([NEG_INF], np.float32).view(np.int32)
_NEG_KEY = int((_NEG_BITS ^ ((_NEG_BITS >> 31) & np.int32(0x7FFFFFFF)))[0])
_INT_MIN = -(2 ** 31)


def _dsa_attn_kernel(q_ref, kv_ref, qi_ref, ki_ref, wi_ref, o_ref,
                     key_ref, bias_ref, s_ref, rmax_ref, acc_ref, *, S):
    top_k = min(DSA_TOP_K_MAX, S // 4)
    idx_bits = (S - 1).bit_length()
    i = pl.program_id(1)
    s0 = i * TQ
    n_chunks = s0 // CK + 1
    lo = _lane_iota((1, LANES)) < 64
    tq = s0 + _lane_iota((1, TQ))

    qi = qi_ref[0]
    qi2 = jnp.concatenate([qi[:, p * LANES:(p + 1) * LANES] for p in range(4)], axis=0)
    zi = jnp.zeros_like(qi2)
    qi_lo = jnp.where(lo, qi2, zi)
    qi_hi = jnp.where(lo, zi, qi2)
    wt = wi_ref[0]

    def idx_body(c, carry):
        off = pl.multiple_of(c * CK, CK)
        ki = ki_ref[0, pl.ds(off, CK), :]
        sa = _dot_nt(ki, qi_lo)
        sb = _dot_nt(ki, qi_hi)
        score = jnp.zeros((CK, TQ), F32)
        for p in range(4):
            cs = slice(p * TQ, (p + 1) * TQ)
            score = score + jnp.maximum(sa[:, cs], 0.0) * wt[2 * p:2 * p + 1, :]
            score = score + jnp.maximum(sb[:, cs], 0.0) * wt[2 * p + 1:2 * p + 2, :]
        kpos = off + lax.broadcasted_iota(jnp.int32, (CK, 1), 0)
        score = jnp.where(kpos <= tq, score, NEG_INF)
        key_ref[pl.ds(off, CK), :] = _sortable(score)
        return carry

    lax.fori_loop(0, n_chunks, idx_body, 0)

    n_out = (S - n_chunks * CK).astype(F32)

    def count(pred_fn):
        def body(c, acc):
            off = pl.multiple_of(c * CK, CK)
            hit = jnp.where(pred_fn(key_ref[pl.ds(off, CK), :], off), 1.0, 0.0)
            return acc + _tree_sum([hit[8 * j:8 * j + 8] for j in range(CK // 8)])
        acc = lax.fori_loop(0, n_chunks, body, jnp.zeros((8, TQ), F32))
        return jnp.sum(acc, axis=0, keepdims=True)

    def bit_body(it, tu):
        bit = jnp.left_shift(jnp.int32(1), 31 - it)
        cand = (tu | bit) ^ jnp.int32(_INT_MIN)
        cnt = count(lambda kk, off: kk >= cand)
        cnt = cnt + n_out * jnp.where(jnp.int32(_NEG_KEY) >= cand, 1.0, 0.0)
        return jnp.where(cnt >= float(top_k), tu | bit, tu)

    tu = lax.fori_loop(0, 32, bit_body, jnp.zeros((1, TQ), jnp.int32))
    thr = tu ^ jnp.int32(_INT_MIN)
    n_gt = count(lambda kk, off: kk > thr)
    n_gt = n_gt + n_out * jnp.where(jnp.int32(_NEG_KEY) > thr, 1.0, 0.0)
    need = float(top_k) - n_gt
    n_eq = count(lambda kk, off: kk == thr)

    def row_iota(off):
        return off + lax.broadcasted_iota(jnp.int32, (CK, 1), 0)

    def tie_search():
        def tie_body(it, jb):
            bit = jnp.left_shift(jnp.int32(1), idx_bits - 1 - it)
            cand = jb | bit
            cnt = count(lambda kk, off: (kk == thr) & (row_iota(off) < cand))
            return jnp.where(cnt < need, cand, jb)
        return lax.fori_loop(0, idx_bits, tie_body, jnp.zeros((1, TQ), jnp.int32))

    jb = lax.cond(jnp.max(n_eq - need) > 0.0, tie_search,
                  lambda: jnp.full((1, TQ), 2 ** idx_bits - 1, jnp.int32))

    eye = jnp.where(lax.broadcasted_iota(jnp.int32, (TQ, TQ), 0) == _lane_iota((TQ, TQ)), 1.0, 0.0).astype(BF16)

    def bias_body(c, carry):
        off = pl.multiple_of(c * CK, CK)
        kk = key_ref[pl.ds(off, CK), :]
        kpos = row_iota(off)
        tie = jnp.where(kk == thr, jnp.where(kpos <= jb, 1.0, 0.0), 0.0)
        picked = jnp.where(kpos <= tq, jnp.where(kk > thr, 1.0, tie), 0.0)
        sel_qk = _dot_nt(eye, picked.astype(BF16))
        bias_ref[:, pl.ds(off, CK)] = (sel_qk - 1.0) * BIG
        return carry

    lax.fori_loop(0, n_chunks, bias_body, 0)

    @pl.when(n_chunks * CK < (s0 // CK_WIDE + 1) * CK_WIDE)
    def _():
        bias_ref[:, pl.ds(pl.multiple_of(n_chunks * CK, CK), CK)] = jnp.full((TQ, CK), NEG_INF, F32)

    q = q_ref[0]
    q2 = jnp.concatenate([q[:, p * LANES:(p + 1) * LANES] for p in range(8)], axis=0)
    zq = jnp.zeros_like(q2)
    for e, qh in enumerate((jnp.where(lo, q2, zq), jnp.where(lo, zq, q2))):
        _masked_softmax_pv([qh],
                           lambda off: kv_ref[0, pl.ds(off, CK_WIDE), 0:LANES],
                           lambda _, off, e=e: kv_ref[0, pl.ds(off, CK_WIDE), (1 + e) * LANES:(2 + e) * LANES],
                           lambda off: bias_ref[:, pl.ds(off, CK_WIDE)],
                           0, s0 // CK_WIDE + 1, s_ref, rmax_ref, acc_ref.at[pl.ds(e, 1)], ck=CK_WIDE)
    o2 = _norm_pair(acc_ref[0], acc_ref[1], lo)
    for p in range(8):
        o_ref[0, :, p * LANES:(p + 1) * LANES] = o2[p * TQ:(p + 1) * TQ].astype(BF16)


def _dsa_attention(q, kv, qi, ki, wi, S):
    B = q.shape[0]
    return pl.pallas_call(
        functools.partial(_dsa_attn_kernel, S=S),
        grid=(B, S // TQ),
        in_specs=[
            pl.BlockSpec((1, TQ, 1024), lambda b, i: (b, i, 0)),
            pl.BlockSpec((1, S, 3 * LANES), lambda b, i: (b, 0, 0)),
            pl.BlockSpec((1, TQ, 512), lambda b, i: (b, i, 0)),
            pl.BlockSpec((1, S, LANES), lambda b, i: (b, 0, 0)),
            pl.BlockSpec((1, IDX_HEADS, TQ), lambda b, i: (b, 0, i)),
        ],
        out_specs=pl.BlockSpec((1, TQ, 1024), lambda b, i: (b, i, 0)),
        out_shape=jax.ShapeDtypeStruct((B, S, 1024), BF16),
        scratch_shapes=[pltpu.VMEM((S, TQ), jnp.int32),
                        pltpu.VMEM((TQ, S), F32),
                        pltpu.VMEM((1, 8 * TQ, S), F32),
                        pltpu.VMEM((1, 8 * TQ, LANES), F32),
                        pltpu.VMEM((2, 8 * TQ, LANES), F32)],
        compiler_params=_cparams(("parallel", "parallel")),
        name="dsa_attention",
    )(q, kv, qi, ki, wi)


def _out_proj_kernel(a_ref, w_ref, x_ref, g_ref, o_ref):
    y = _dot(a_ref[...], w_ref[...])
    o_ref[...] = x_ref[...] + _rms(y, g_ref[...])


def _out_proj(a2d, w, x2d, g):
    T = x2d.shape[0]
    tm = 512
    return pl.pallas_call(
        _out_proj_kernel,
        grid=(T // tm,),
        in_specs=[
            pl.BlockSpec((tm, 1024), lambda i: (i, 0)),
            pl.BlockSpec((1024, D_MODEL), lambda i: (0, 0)),
            pl.BlockSpec((tm, D_MODEL), lambda i: (i, 0)),
            pl.BlockSpec((1, D_MODEL), lambda i: (0, 0)),
        ],
        out_specs=pl.BlockSpec((tm, D_MODEL), lambda i: (i, 0)),
        out_shape=jax.ShapeDtypeStruct((T, D_MODEL), F32),
        compiler_params=_cparams(("parallel",)),
        name="mixer_out_proj",
    )(a2d, w, x2d, g)


FFN_TM = 512
FFN_FC = 256
FFN_NF = D_FF // FFN_FC
FFN_HALO = 16
FFN_DOWN_GROUP = 4


def _ffn_kernel(x_ref, xh_ref, g2_ref, g3_ref, wup_ref, cw_ref, cb_ref, wd_ref, o_ref, hext, ubuf, act_ref,
                *, tiles_per_seq):
    i = pl.program_id(0)
    tm = FFN_TM
    g2 = g2_ref[...]
    x = x_ref[...]
    hext[FFN_HALO:, :] = _rms(x, g2).astype(BF16)
    hh = _rms(xh_ref[...], g2)
    hh = jnp.where(i % tiles_per_seq == 0, 0.0, hh)
    hext[0:FFN_HALO, :] = hh.astype(BF16)
    he = hext[...]

    def conv(u_ref, col):
        w = cw_ref[:, col:col + FFN_FC]
        return (w[2:3] * u_ref[FFN_HALO:FFN_HALO + tm, :]
                + w[1:2] * u_ref[FFN_HALO - 1:FFN_HALO - 1 + tm, :]
                + w[0:1] * u_ref[FFN_HALO - 2:FFN_HALO - 2 + tm, :]
                + cb_ref[:, col:col + FFN_FC])

    y = None
    for f in range(FFN_NF):
        cv = f * FFN_FC
        cg = D_FF + f * FFN_FC
        uv = ubuf.at[(2 * f) % 4]
        ug = ubuf.at[(2 * f + 1) % 4]
        uv[...] = _dot(he, wup_ref[:, cv:cv + FFN_FC])
        ug[...] = _dot(he, wup_ref[:, cg:cg + FFN_FC])
        act_ref[:, cv:cv + FFN_FC] = (_gelu_tanh(conv(ug, cg)) * conv(uv, cv)).astype(BF16)
        if (f + 1) % FFN_DOWN_GROUP == 0 or f == FFN_NF - 1:
            k0 = (f // FFN_DOWN_GROUP) * FFN_DOWN_GROUP * FFN_FC
            part = _dot(act_ref[:, k0:cv + FFN_FC], wd_ref[k0:cv + FFN_FC, :])
            y = part if y is None else y + part
    o_ref[...] = x + _rms(y, g3_ref[...])


def _ffn(x2d, g2, g3, wup, cw, cb, wd, S):
    T = x2d.shape[0]
    tm = FFN_TM
    hb = tm // FFN_HALO
    const = lambda i: (0, 0)
    return pl.pallas_call(
        functools.partial(_ffn_kernel, tiles_per_seq=S // tm),
        grid=(T // tm,),
        in_specs=[
            pl.BlockSpec((tm, D_MODEL), lambda i: (i, 0)),
            pl.BlockSpec((FFN_HALO, D_MODEL), lambda i: (jnp.maximum(i * hb - 1, 0), 0)),
            pl.BlockSpec((1, D_MODEL), const),
            pl.BlockSpec((1, D_MODEL), const),
            pl.BlockSpec((D_MODEL, 2 * D_FF), const),
            pl.BlockSpec((CONV_WIDTH, 2 * D_FF), const),
            pl.BlockSpec((1, 2 * D_FF), const),
            pl.BlockSpec((D_FF, D_MODEL), const),
        ],
        out_specs=pl.BlockSpec((tm, D_MODEL), lambda i: (i, 0)),
        out_shape=jax.ShapeDtypeStruct((T, D_MODEL), F32),
        scratch_shapes=[
            pltpu.VMEM((tm + FFN_HALO, D_MODEL), BF16),
            pltpu.VMEM((4, tm + FFN_HALO, FFN_FC), F32),
            pltpu.VMEM((tm, D_FF), BF16),
        ],
        compiler_params=_cparams(("parallel",)),
        name="conv_ffn",
    )(x2d, x2d, g2, g3, wup, cw, cb, wd)


def _rope_tables(pos):
    half = HEAD_DIM // 2
    inv = ROPE_THETA ** (-jnp.arange(half, dtype=F32) / half)
    ang = pos.astype(F32)[:, None] * inv[None, :]
    cos, sin = jnp.cos(ang), jnp.sin(ang)
    return jnp.tile(cos, (1, 4)), jnp.tile(jnp.concatenate([-sin, sin], axis=-1), (1, 2))


def _tables(S):
    ncp = S // CMP_STRIDE
    n_sel = S // SEL_BLOCK
    cos2, sin2 = _rope_tables(jnp.arange(S, dtype=jnp.int32))
    cosk, sink = _rope_tables(jnp.arange(ncp, dtype=jnp.int32) * CMP_STRIDE + (CMP_BLOCK - 1))
    cosc = jnp.stack([cosk, jnp.ones_like(cosk)])
    sinc = jnp.stack([sink, jnp.zeros_like(sink)])
    sj = np.arange(n_sel)[:, None]
    ci = np.arange(ncp)[None, :]
    agg_t = ((ci * CMP_STRIDE <= sj * SEL_BLOCK + SEL_BLOCK - 1)
             & (ci * CMP_STRIDE + CMP_BLOCK - 1 >= sj * SEL_BLOCK)
             & (ci < (S - CMP_BLOCK) // CMP_STRIDE + 1))
    agg_t = jnp.asarray(agg_t, dtype=BF16)
    emat = jnp.asarray(np.arange(LANES)[:, None] == (np.arange(S)[None, :] // SEL_BLOCK), dtype=BF16)
    return (cos2, sin2, cosc, sinc, agg_t, emat)


def _pad_cols(w, width):
    return jnp.pad(w, ((0, 0), (0, width - w.shape[1])))


def _nsa_mixer(x2d, g_pre, g_post, w_in, pe_k, w1_k, w2_k, pe_v, w1_v, w2_v, w_out, B, S, tabs):
    cos2, sin2, cosc, sinc, agg, emat = tabs
    ncp = S // CMP_STRIDE
    q, kv, cv, gate = _nsa_in_proj(x2d, g_pre, _pad_cols(w_in, IN_W_PAD).astype(BF16), cos2, sin2, S)
    chunks = cv.reshape(B, ncp, CMP_STRIDE, 2, NSA_KV_GROUPS, HEAD_DIM).transpose(0, 3, 4, 1, 2, 5)
    chunks = chunks.reshape(B, 4, ncp, CMP_STRIDE * HEAD_DIM)
    nxt = jnp.concatenate([chunks[:, :, 1:], chunks[:, :, :1]], axis=2)
    blocks = jnp.concatenate([chunks, nxt], axis=-1)
    pe = jnp.stack([pe_k.reshape(1, -1), pe_v.reshape(1, -1)])
    w1 = jnp.stack([w1_k, w1_v]).astype(BF16)
    w2d = jnp.stack([jnp.tile(w2_k, (1, 2)), jnp.tile(w2_v, (1, 2))]).astype(BF16)
    cmp4 = _nsa_compress(blocks, pe, w1, w2d, cosc, sinc)
    ones = jnp.ones((B, ncp, HEAD_DIM), BF16)
    cmpkv = jnp.concatenate(
        [piece for g in range(NSA_KV_GROUPS) for piece in (
            cmp4[:, :, g * LANES:(g + 1) * LANES],
            cmp4[:, :, (2 + g) * LANES:(2 + g) * LANES + HEAD_DIM], ones,
            ones, cmp4[:, :, (2 + g) * LANES:(2 + g) * LANES + HEAD_DIM])], axis=-1)
    o = _nsa_attention(q.reshape(B, S, 1024), kv.reshape(B, S, 12 * LANES), cmpkv,
                       gate.reshape(B, S, 256), agg, emat, S)
    return _out_proj(o.reshape(B * S, 1024), w_out.astype(BF16), x2d, g_post)


def _dsa_mixer(x2d, g_pre, g_post, w_in, kv_norm, w_uk, w_uv, w_out, B, S, tabs):
    cos2, sin2 = tabs[0], tabs[1]
    wukv = jnp.concatenate([w_uk, w_uv], axis=1).astype(BF16)
    q, kv, qi, ki, wi = _dsa_in_proj(x2d, g_pre, _pad_cols(w_in, IN_W_PAD).astype(BF16),
                                     kv_norm.reshape(1, -1), wukv, cos2, sin2, S)
    o = _dsa_attention(q.reshape(B, S, 1024), kv.reshape(B, S, 3 * LANES), qi.reshape(B, S, 512),
                       ki.reshape(B, S, LANES), wi, S)
    return _out_proj(o.reshape(B * S, 1024), w_out.astype(BF16), x2d, g_post)


def kernel(x, norm_g, nsa_w_in, nsa_cmp_pe_k, nsa_cmp_w1_k, nsa_cmp_w2_k, nsa_cmp_pe_v, nsa_cmp_w1_v, nsa_cmp_w2_v, nsa_w_out, dsa_w_in, dsa_kv_norm, dsa_w_uk, dsa_w_uv, dsa_w_out, ffn_w_up, ffn_conv_w, ffn_conv_b, ffn_w_down):
    B, S, _ = x.shape
    tabs = _tables(S)
    x2d = x.reshape(B * S, D_MODEL)
    for i in range(DEPTH):
        g = norm_g[i].reshape(4, 1, D_MODEL)
        j = i // 2
        if i % 2 == 0:
            x2d = _nsa_mixer(x2d, g[0], g[1], nsa_w_in[j], nsa_cmp_pe_k[j], nsa_cmp_w1_k[j], nsa_cmp_w2_k[j],
                             nsa_cmp_pe_v[j], nsa_cmp_w1_v[j], nsa_cmp_w2_v[j], nsa_w_out[j], B, S, tabs)
        else:
            x2d = _dsa_mixer(x2d, g[0], g[1], dsa_w_in[j], dsa_kv_norm[j], dsa_w_uk[j], dsa_w_uv[j],
                             dsa_w_out[j], B, S, tabs)
        x2d = _ffn(x2d, g[2], g[3], ffn_w_up[i].astype(BF16), ffn_conv_w[i],
                   ffn_conv_b[i].reshape(1, -1), ffn_w_down[i].astype(BF16), S)
    return x2d.reshape(B, S, D_MODEL)
```

```python
import functools

import numpy as np
import jax
import jax.numpy as jnp
from jax import lax
from jax.experimental import pallas as pl
from jax.experimental.pallas import tpu as pltpu

F32 = jnp.float32
BF16 = jnp.bfloat16

D_MODEL = 1024
DEPTH = 4
N_HEADS = 16
HEAD_DIM = 64
ROPE_THETA = 10000.0
RMS_EPS = 1e-6
NEG_INF = -1e30
BIG = 1e30

NSA_KV_GROUPS = 2
CMP_BLOCK = 32
CMP_STRIDE = 16
CMP_HIDDEN = 256
SEL_BLOCK = 64
SEL_TOP_N = 16
WINDOW = 512

DSA_KV_RANK = 256
IDX_HEADS = 8
IDX_DIM = 64
DSA_TOP_K_MAX = 256

D_FF = 2816
CONV_WIDTH = 3

NSA_IN_W = N_HEADS * HEAD_DIM + 6 * NSA_KV_GROUPS * HEAD_DIM + 3 * N_HEADS
DSA_IN_W = N_HEADS * HEAD_DIM + DSA_KV_RANK + IDX_HEADS * IDX_DIM + IDX_DIM + IDX_HEADS

Q_SCALE = HEAD_DIM ** -0.5 * 1.4426950408889634

LANES = 128
IN_W_PAD = 1920
IN_PROJ_TM = 512
TQ = 256
CK = 256
CK_WIDE = 512
VMEM_LIMIT = 56 * 1024 * 1024


def _cparams(sem):
    return pltpu.CompilerParams(dimension_semantics=sem, vmem_limit_bytes=VMEM_LIMIT)


def _rms(x, g):
    return x * lax.rsqrt(jnp.mean(x * x, axis=-1, keepdims=True) + RMS_EPS) * g


def _lane_iota(shape):
    return lax.broadcasted_iota(jnp.int32, shape, len(shape) - 1)


def _rope128(x, cos2, sin2):
    lane = _lane_iota(x.shape)
    first = (lane % 64) < 32
    swapped = jnp.where(first, pltpu.roll(x, 96, axis=1), pltpu.roll(x, 32, axis=1))
    return x * cos2 + swapped * sin2


def _dup_halves(x):
    lo = _lane_iota(x.shape) < 64
    r = pltpu.roll(x, 64, axis=1)
    return jnp.where(lo, x, r), jnp.where(lo, r, x)


def _dot(a, b):
    return jnp.dot(a, b, preferred_element_type=F32)


def _dot_nt(a, b):
    return lax.dot_general(a, b, (((1,), (1,)), ((), ())), preferred_element_type=F32)


def _tree_sum(parts):
    while len(parts) > 1:
        parts = [parts[j] + parts[j + 1] if j + 1 < len(parts) else parts[j] for j in range(0, len(parts), 2)]
    return parts[0]


def _gelu_tanh(x):
    return 0.5 * x * (1.0 + jnp.tanh(0.7978845608028654 * (x + 0.044715 * (x * x * x))))


def _ones_halves(x):
    lo = _lane_iota(x.shape) < 64
    return jnp.where(lo, x, 1.0), jnp.where(lo, 1.0, x)


def _masked_softmax_pv(q_halves, k_of, v_of, bias_of, c_lo, c_hi, s_ref, rmax_ref, acc_ref, ck=CK):
    n_e = len(q_halves)
    rows = q_halves[0].shape[0]
    r = rows // TQ
    for e in range(n_e):
        rmax_ref[e] = jnp.full((rows, LANES), NEG_INF, F32)
        acc_ref[e] = jnp.zeros((rows, LANES), F32)

    def pass1(c, carry):
        off = pl.multiple_of(c * ck, ck)
        k = k_of(off)
        bias = bias_of(off)
        for e in range(n_e):
            for j in range(r):
                rs = slice(j * TQ, (j + 1) * TQ)
                s = _dot_nt(q_halves[e][rs], k) + bias
                s_ref[e, rs, pl.ds(off, ck)] = s
                cols = [s[:, i * LANES:(i + 1) * LANES] for i in range(ck // LANES)]
                rmax_ref[e, rs] = functools.reduce(jnp.maximum, cols, rmax_ref[e, rs])
        return carry

    lax.fori_loop(c_lo, c_hi, pass1, 0)
    for e in range(n_e):
        m = jnp.max(rmax_ref[e], axis=-1, keepdims=True)
        m = jnp.where(m > NEG_INF, m, 0.0)
        rmax_ref[e] = jnp.broadcast_to(m, (rows, LANES))

    def pass2(c, carry):
        off = pl.multiple_of(c * ck, ck)
        for e in range(n_e):
            v = v_of(e, off)
            for j in range(r):
                rs = slice(j * TQ, (j + 1) * TQ)
                mb = rmax_ref[e, rs]
                p = jnp.exp2(s_ref[e, rs, pl.ds(off, ck)] - jnp.concatenate([mb] * (ck // LANES), axis=1))
                acc_ref[e, rs] += _dot(p.astype(BF16), v)
        return carry

    lax.fori_loop(c_lo, c_hi, pass2, 0)


def _norm_pair(acc_a, acc_b, lo):
    num = jnp.where(lo, acc_a, acc_b)
    den = pltpu.roll(jnp.where(lo, acc_b, acc_a), 64, axis=1)
    return num / jnp.where(den > 0.0, den, 1.0)


def _nsa_in_kernel(x_ref, g_ref, w_ref, cos_ref, sin_ref, q_ref, kv_ref, kc_ref, vc_ref, gate_ref):
    h = _rms(x_ref[...], g_ref[...]).astype(BF16)
    proj = _dot(h, w_ref[...])
    cos2 = cos_ref[...]
    sin2 = sin_ref[...]
    for m in range(8):
        slab = proj[:, m * LANES:(m + 1) * LANES]
        q_ref[:, m * LANES:(m + 1) * LANES] = (_rope128(slab, cos2, sin2) * Q_SCALE).astype(BF16)
    kc_ref[...] = proj[:, 1024:1152]
    vc_ref[...] = proj[:, 1152:1280]
    ks = _rope128(proj[:, 1280:1408], cos2, sin2)
    vs = proj[:, 1408:1536]
    kw = _rope128(proj[:, 1536:1664], cos2, sin2)
    vw = proj[:, 1664:1792]
    for j, (kk, vv) in enumerate(((ks, vs), (kw, vw))):
        k0, k1 = _dup_halves(kk)
        v0, v1 = _dup_halves(vv)
        for g, (kd, vd) in enumerate(((k0, v0), (k1, v1))):
            base = (g * 6 + j * 3) * LANES
            v_lo, v_hi = _ones_halves(vd)
            kv_ref[:, base:base + LANES] = kd.astype(BF16)
            kv_ref[:, base + LANES:base + 2 * LANES] = v_lo.astype(BF16)
            kv_ref[:, base + 2 * LANES:base + 3 * LANES] = v_hi.astype(BF16)
    gt = proj[:, 1792:1920]
    sg = 1.0 / (1.0 + jnp.exp(-gt))
    gate_ref[:, 0:LANES] = sg
    gate_ref[:, LANES:2 * LANES] = pltpu.roll(sg, LANES - 24, axis=1)


def _nsa_in_proj(x2d, g, w, cos2, sin2, S):
    T = x2d.shape[0]
    tm = IN_PROJ_TM
    tps = S // tm
    return pl.pallas_call(
        _nsa_in_kernel,
        grid=(T // tm,),
        in_specs=[
            pl.BlockSpec((tm, D_MODEL), lambda i: (i, 0)),
            pl.BlockSpec((1, D_MODEL), lambda i: (0, 0)),
            pl.BlockSpec((D_MODEL, IN_W_PAD), lambda i: (0, 0)),
            pl.BlockSpec((tm, LANES), lambda i: (i % tps, 0)),
            pl.BlockSpec((tm, LANES), lambda i: (i % tps, 0)),
        ],
        out_specs=[
            pl.BlockSpec((tm, 1024), lambda i: (i, 0)),
            pl.BlockSpec((tm, 12 * LANES), lambda i: (i, 0)),
            pl.BlockSpec((tm, LANES), lambda i: (i, 0)),
            pl.BlockSpec((tm, LANES), lambda i: (i, 0)),
            pl.BlockSpec((tm, 256), lambda i: (i, 0)),
        ],
        out_shape=[
            jax.ShapeDtypeStruct((T, 1024), BF16),
            jax.ShapeDtypeStruct((T, 12 * LANES), BF16),
            jax.ShapeDtypeStruct((T, LANES), F32),
            jax.ShapeDtypeStruct((T, LANES), F32),
            jax.ShapeDtypeStruct((T, 256), F32),
        ],
        compiler_params=_cparams(("parallel",)),
        name="nsa_in_proj",
    )(x2d, g, w, cos2, sin2)


def _nsa_cmp_kernel(kc_ref, vc_ref, pe_ref, w1_ref, w2_ref, cos_ref, sin_ref, o_ref):
    ncp = o_ref.shape[1]
    for kv, src in enumerate((kc_ref, vc_ref)):
        parts = []
        for l in range(CMP_STRIDE):
            xl = src[0, pl.ds(l, ncp, stride=CMP_STRIDE), :]
            xn = pltpu.roll(xl, ncp - 1, axis=0)
            parts.append(_dot((xl + pe_ref[kv, l:l + 1, :]).astype(BF16), w1_ref[kv, l]))
            ln = CMP_STRIDE + l
            parts.append(_dot((xn + pe_ref[kv, ln:ln + 1, :]).astype(BF16), w1_ref[kv, ln]))
        hid = _gelu_tanh(_tree_sum(parts))
        out = _dot(hid.astype(BF16), w2_ref[kv])
        for g in range(NSA_KV_GROUPS):
            slab = out[:, g * LANES:(g + 1) * LANES]
            if kv == 0:
                slab = _rope128(slab, cos_ref[...], sin_ref[...])
            j = kv * NSA_KV_GROUPS + g
            o_ref[0, :, j * LANES:(j + 1) * LANES] = slab.astype(BF16)


def _nsa_compress(kc, vc, pe2, w1g, w2g, cosk, sink):
    B, S, _ = kc.shape
    ncp = S // CMP_STRIDE
    full = lambda *shape: pl.BlockSpec(shape, lambda b: (0,) * len(shape))
    return pl.pallas_call(
        _nsa_cmp_kernel,
        grid=(B,),
        in_specs=[
            pl.BlockSpec((1, S, LANES), lambda b: (b, 0, 0)),
            pl.BlockSpec((1, S, LANES), lambda b: (b, 0, 0)),
            full(2, CMP_BLOCK, LANES),
            full(2, CMP_BLOCK, LANES, 2 * CMP_HIDDEN),
            full(2, 2 * CMP_HIDDEN, 2 * LANES),
            full(ncp, LANES),
            full(ncp, LANES),
        ],
        out_specs=pl.BlockSpec((1, ncp, 4 * LANES), lambda b: (b, 0, 0)),
        out_shape=jax.ShapeDtypeStruct((B, ncp, 4 * LANES), BF16),
        compiler_params=_cparams(("parallel",)),
        name="nsa_compress",
    )(kc, vc, pe2, w1g, w2g, cosk, sink)


def _nsa_attn_kernel(q_ref, kv_ref, cmp_ref, gate_ref, agg_ref, e_ref, o_ref,
                     s_ref, rmax_ref, acc_ref, *, S):
    n_sel = S // SEL_BLOCK
    top_n = min(SEL_TOP_N, n_sel)
    i = pl.program_id(2)
    s0 = i * TQ
    lo = _lane_iota((1, LANES)) < 64
    q = q_ref[0]
    q2 = jnp.concatenate([q[:, p * LANES:(p + 1) * LANES] for p in range(4)], axis=0)
    zero = jnp.zeros_like(q2)
    q_lo = jnp.where(lo, q2, zero)
    q_hi = jnp.where(lo, zero, q2)
    t = s0 + lax.broadcasted_iota(jnp.int32, (TQ, 1), 0)

    kc = cmp_ref[0, :, 0:LANES]
    ncp = kc.shape[0]
    cend = _lane_iota((1, ncp)) * CMP_STRIDE + (CMP_BLOCK - 1)
    mask_c = (cend <= t)[None]

    def cmp_branch(qh, vc1):
        s3 = jnp.where(mask_c, _dot_nt(qh, kc).reshape(4, TQ, ncp), NEG_INF)
        e = jnp.where(mask_c, jnp.exp2(s3 - jnp.max(s3, axis=-1, keepdims=True)), 0.0)
        e = e.reshape(4 * TQ, ncp)
        return e, _dot(e.astype(BF16), vc1)

    ea, acc_ca = cmp_branch(q_lo, cmp_ref[0, :, LANES:2 * LANES])
    eb, acc_cb = cmp_branch(q_hi, cmp_ref[0, :, 2 * LANES:3 * LANES])
    oc2 = _norm_pair(acc_ca, acc_cb, lo)
    den = pltpu.roll(jnp.where(lo, acc_cb, acc_ca), 64, axis=1)
    l_a = jnp.where(lo, den, acc_ca)
    l_b = jnp.where(lo, acc_cb, den)
    pa = ea / jnp.where(l_a > 0.0, l_a, 1.0)
    pb = eb / jnp.where(l_b > 0.0, l_b, 1.0)
    psum = _tree_sum([pa[p * TQ:(p + 1) * TQ] for p in range(4)] + [pb[p * TQ:(p + 1) * TQ] for p in range(4)])
    agg = agg_ref[...]
    p_hi = psum.astype(BF16)
    r1 = psum - p_hi.astype(F32)
    p_mid = r1.astype(BF16)
    p_lo = (r1 - p_mid.astype(F32)).astype(BF16)
    imp = _dot_nt(agg, p_hi) + _dot_nt(agg, p_mid) + _dot_nt(agg, p_lo)

    tq = s0 + _lane_iota((1, TQ))
    jblk = lax.broadcasted_iota(jnp.int32, (n_sel, 1), 0)
    cur = jnp.right_shift(tq, SEL_BLOCK.bit_length() - 1)
    forced = (jblk == 0) | (jblk == cur) | (jblk == cur - 1)
    causal_blk = jblk * SEL_BLOCK <= tq
    impm = jnp.where(causal_blk, jnp.where(forced, BIG, imp), NEG_INF)
    beats = []
    for k in range(n_sel):
        row = impm[k:k + 1, :]
        tie_k = jnp.where(jblk > k, 1.0, 0.0)
        beats.append(jnp.where(row > impm, 1.0, jnp.where(row == impm, tie_k, 0.0)))
    rank = _tree_sum(beats)
    sel_t = jnp.where(rank < float(top_n), 1.0, 0.0)
    sel = jnp.concatenate([sel_t, jnp.zeros((LANES - n_sel, TQ), F32)], axis=0).T.astype(BF16)
    q_halves = [q_lo, q_hi]
    c_hi = s0 // CK + 1

    def sel_bias(off):
        kpos = off + _lane_iota((1, CK_WIDE))
        picked = _dot(sel, e_ref[:, pl.ds(off, CK_WIDE)])
        return jnp.where(picked > 0.5, jnp.where(kpos <= t, 0.0, NEG_INF), NEG_INF)

    _masked_softmax_pv(q_halves,
                       lambda off: kv_ref[0, pl.ds(off, CK_WIDE), 0:LANES],
                       lambda e, off: kv_ref[0, pl.ds(off, CK_WIDE), (1 + e) * LANES:(2 + e) * LANES],
                       sel_bias, 0, s0 // CK_WIDE + 1, s_ref, rmax_ref, acc_ref, ck=CK_WIDE)
    os2 = _norm_pair(acc_ref[0], acc_ref[1], lo)

    def win_bias(off):
        dist = t - (off + _lane_iota((1, CK)))
        return jnp.where(dist >= 0, jnp.where(dist < WINDOW, 0.0, NEG_INF), NEG_INF)

    c_lo = jnp.maximum(s0 - (WINDOW - 1), 0) // CK
    _masked_softmax_pv(q_halves,
                       lambda off: kv_ref[0, pl.ds(off, CK), 3 * LANES:4 * LANES],
                       lambda e, off: kv_ref[0, pl.ds(off, CK), (4 + e) * LANES:(5 + e) * LANES],
                       win_bias, c_lo, c_hi, s_ref, rmax_ref, acc_ref)
    ow2 = _norm_pair(acc_ref[0], acc_ref[1], lo)

    gt = gate_ref[0]
    for p in range(4):
        rs = slice(p * TQ, (p + 1) * TQ)
        ca, cb = (2 * p) * 3, (2 * p + 1) * 3
        g_c = jnp.where(lo, gt[:, ca:ca + 1], gt[:, cb:cb + 1])
        g_s = jnp.where(lo, gt[:, ca + 1:ca + 2], gt[:, cb + 1:cb + 2])
        g_w = jnp.where(lo, gt[:, ca + 2:ca + 3], gt[:, cb + 2:cb + 3])
        o_ref[0, :, p * LANES:(p + 1) * LANES] = (g_c * oc2[rs] + g_s * os2[rs] + g_w * ow2[rs]).astype(BF16)


def _nsa_attention(q, kv, cmpkv, gate, agg, emat, S):
    B = q.shape[0]
    ncp = cmpkv.shape[1]
    return pl.pallas_call(
        functools.partial(_nsa_attn_kernel, S=S),
        grid=(B, NSA_KV_GROUPS, S // TQ),
        in_specs=[
            pl.BlockSpec((1, TQ, 512), lambda b, g, i: (b, i, g)),
            pl.BlockSpec((1, S, 6 * LANES), lambda b, g, i: (b, 0, g)),
            pl.BlockSpec((1, ncp, 3 * LANES), lambda b, g, i: (b, 0, g)),
            pl.BlockSpec((1, TQ, LANES), lambda b, g, i: (b, i, g)),
            pl.BlockSpec((S // SEL_BLOCK, ncp), lambda b, g, i: (0, 0)),
            pl.BlockSpec((LANES, S), lambda b, g, i: (0, 0)),
        ],
        out_specs=pl.BlockSpec((1, TQ, 512), lambda b, g, i: (b, i, g)),
        out_shape=jax.ShapeDtypeStruct((B, S, 1024), BF16),
        scratch_shapes=[pltpu.VMEM((2, 4 * TQ, S), F32),
                        pltpu.VMEM((2, 4 * TQ, LANES), F32),
                        pltpu.VMEM((2, 4 * TQ, LANES), F32)],
        compiler_params=_cparams(("parallel", "parallel", "parallel")),
        name="nsa_attention",
    )(q, kv, cmpkv, gate, agg, emat)


def _dsa_in_kernel(x_ref, g_ref, w_ref, kvn_ref, wukv_ref, cos_ref, sin_ref,
                   q_ref, kv_ref, qi_ref, ki_ref, wi_ref):
    h = _rms(x_ref[...], g_ref[...]).astype(BF16)
    proj = _dot(h, w_ref[...])
    cos2 = cos_ref[...]
    sin2 = sin_ref[...]
    for m in range(8):
        slab = proj[:, m * LANES:(m + 1) * LANES]
        q_ref[:, m * LANES:(m + 1) * LANES] = (_rope128(slab, cos2, sin2) * Q_SCALE).astype(BF16)
    c = _rms(proj[:, 1024:1280], kvn_ref[...]).astype(BF16)
    kvu = _dot(c, wukv_ref[...])
    kd, vd = _dup_halves(kvu)
    v_lo, v_hi = _ones_halves(vd)
    kv_ref[:, 0:LANES] = _rope128(kd, cos2, sin2).astype(BF16)
    kv_ref[:, LANES:2 * LANES] = v_lo.astype(BF16)
    kv_ref[:, 2 * LANES:3 * LANES] = v_hi.astype(BF16)
    for m in range(4):
        slab = proj[:, 1280 + m * LANES:1280 + (m + 1) * LANES]
        qi_ref[:, m * LANES:(m + 1) * LANES] = (_rope128(slab, cos2, sin2) * (IDX_DIM ** -0.5)).astype(BF16)
    last = proj[:, 1792:1920]
    kid, wid = _dup_halves(last)
    ki_ref[...] = _rope128(kid, cos2, sin2).astype(BF16)
    wi_ref[0] = (wid * (IDX_HEADS ** -0.5)).T[0:IDX_HEADS, :]


def _dsa_in_proj(x2d, g, w, kvn, wukv, cos2, sin2, S):
    T = x2d.shape[0]
    tm = IN_PROJ_TM
    tps = S // tm
    return pl.pallas_call(
        _dsa_in_kernel,
        grid=(T // tm,),
        in_specs=[
            pl.BlockSpec((tm, D_MODEL), lambda i: (i, 0)),
            pl.BlockSpec((1, D_MODEL), lambda i: (0, 0)),
            pl.BlockSpec((D_MODEL, IN_W_PAD), lambda i: (0, 0)),
            pl.BlockSpec((1, DSA_KV_RANK), lambda i: (0, 0)),
            pl.BlockSpec((DSA_KV_RANK, LANES), lambda i: (0, 0)),
            pl.BlockSpec((tm, LANES), lambda i: (i % tps, 0)),
            pl.BlockSpec((tm, LANES), lambda i: (i % tps, 0)),
        ],
        out_specs=[
            pl.BlockSpec((tm, 1024), lambda i: (i, 0)),
            pl.BlockSpec((tm, 3 * LANES), lambda i: (i, 0)),
            pl.BlockSpec((tm, 512), lambda i: (i, 0)),
            pl.BlockSpec((tm, LANES), lambda i: (i, 0)),
            pl.BlockSpec((1, IDX_HEADS, tm), lambda i: (i // tps, 0, i % tps)),
        ],
        out_shape=[
            jax.ShapeDtypeStruct((T, 1024), BF16),
            jax.ShapeDtypeStruct((T, 3 * LANES), BF16),
            jax.ShapeDtypeStruct((T, 512), BF16),
            jax.ShapeDtypeStruct((T, LANES), BF16),
            jax.ShapeDtypeStruct((T // S, IDX_HEADS, S), F32),
        ],
        compiler_params=_cparams(("parallel",)),
        name="dsa_in_proj",
    )(x2d, g, w, kvn, wukv, cos2, sin2)


def _sortable(v):
    bits = lax.bitcast_convert_type(v + 0.0, jnp.int32)
    return bits ^ ((bits >> 31) & jnp.int32(0x7FFFFFFF))


_NEG_BITS = np.array([NEG_INF], np.float32).view(np.int32)
_NEG_KEY = int((_NEG_BITS ^ ((_NEG_BITS >> 31) & np.int32(0x7FFFFFFF)))[0])
_INT_MIN = -(2 ** 31)


def _dsa_attn_kernel(q_ref, kv_ref, qi_ref, ki_ref, wi_ref, o_ref,
                     key_ref, bias_ref, s_ref, rmax_ref, acc_ref, *, S):
    top_k = min(DSA_TOP_K_MAX, S // 4)
    idx_bits = (S - 1).bit_length()
    i = pl.program_id(1)
    s0 = i * TQ
    n_chunks = s0 // CK + 1
    lo = _lane_iota((1, LANES)) < 64
    tq = s0 + _lane_iota((1, TQ))

    qi = qi_ref[0]
    qi2 = jnp.concatenate([qi[:, p * LANES:(p + 1) * LANES] for p in range(4)], axis=0)
    zi = jnp.zeros_like(qi2)
    qi_lo = jnp.where(lo, qi2, zi)
    qi_hi = jnp.where(lo, zi, qi2)
    wt = wi_ref[0]

    def idx_body(c, carry):
        off = pl.multiple_of(c * CK, CK)
        ki = ki_ref[0, pl.ds(off, CK), :]
        sa = _dot_nt(ki, qi_lo)
        sb = _dot_nt(ki, qi_hi)
        score = jnp.zeros((CK, TQ), F32)
        for p in range(4):
            cs = slice(p * TQ, (p + 1) * TQ)
            score = score + jnp.maximum(sa[:, cs], 0.0) * wt[2 * p:2 * p + 1, :]
            score = score + jnp.maximum(sb[:, cs], 0.0) * wt[2 * p + 1:2 * p + 2, :]
        kpos = off + lax.broadcasted_iota(jnp.int32, (CK, 1), 0)
        score = jnp.where(kpos <= tq, score, NEG_INF)
        key_ref[pl.ds(off, CK), :] = _sortable(score)
        return carry

    lax.fori_loop(0, n_chunks, idx_body, 0)

    n_out = (S - n_chunks * CK).astype(F32)

    def count(pred_fn):
        def body(c, acc):
            off = pl.multiple_of(c * CK, CK)
            hit = jnp.where(pred_fn(key_ref[pl.ds(off, CK), :], off), 1.0, 0.0)
            return acc + _tree_sum([hit[8 * j:8 * j + 8] for j in range(CK // 8)])
        acc = lax.fori_loop(0, n_chunks, body, jnp.zeros((8, TQ), F32))
        return jnp.sum(acc, axis=0, keepdims=True)

    def bit_body(it, tu):
        bit = jnp.left_shift(jnp.int32(1), 31 - it)
        cand = (tu | bit) ^ jnp.int32(_INT_MIN)
        cnt = count(lambda kk, off: kk >= cand)
        cnt = cnt + n_out * jnp.where(jnp.int32(_NEG_KEY) >= cand, 1.0, 0.0)
        return jnp.where(cnt >= float(top_k), tu | bit, tu)

    tu = lax.fori_loop(0, 32, bit_body, jnp.zeros((1, TQ), jnp.int32))
    thr = tu ^ jnp.int32(_INT_MIN)
    n_gt = count(lambda kk, off: kk > thr)
    n_gt = n_gt + n_out * jnp.where(jnp.int32(_NEG_KEY) > thr, 1.0, 0.0)
    need = float(top_k) - n_gt
    n_eq = count(lambda kk, off: kk == thr)

    def row_iota(off):
        return off + lax.broadcasted_iota(jnp.int32, (CK, 1), 0)

    def tie_search():
        def tie_body(it, jb):
            bit = jnp.left_shift(jnp.int32(1), idx_bits - 1 - it)
            cand = jb | bit
            cnt = count(lambda kk, off: (kk == thr) & (row_iota(off) < cand))
            return jnp.where(cnt < need, cand, jb)
        return lax.fori_loop(0, idx_bits, tie_body, jnp.zeros((1, TQ), jnp.int32))

    jb = lax.cond(jnp.max(n_eq - need) > 0.0, tie_search,
                  lambda: jnp.full((1, TQ), 2 ** idx_bits - 1, jnp.int32))

    eye = jnp.where(lax.broadcasted_iota(jnp.int32, (TQ, TQ), 0) == _lane_iota((TQ, TQ)), 1.0, 0.0).astype(BF16)

    def bias_body(c, carry):
        off = pl.multiple_of(c * CK, CK)
        kk = key_ref[pl.ds(off, CK), :]
        kpos = row_iota(off)
        tie = jnp.where(kk == thr, jnp.where(kpos <= jb, 1.0, 0.0), 0.0)
        picked = jnp.where(kpos <= tq, jnp.where(kk > thr, 1.0, tie), 0.0)
        sel_qk = _dot_nt(eye, picked.astype(BF16))
        bias_ref[:, pl.ds(off, CK)] = (sel_qk - 1.0) * BIG
        return carry

    lax.fori_loop(0, n_chunks, bias_body, 0)

    @pl.when(n_chunks * CK < (s0 // CK_WIDE + 1) * CK_WIDE)
    def _():
        bias_ref[:, pl.ds(pl.multiple_of(n_chunks * CK, CK), CK)] = jnp.full((TQ, CK), NEG_INF, F32)

    q = q_ref[0]
    q2 = jnp.concatenate([q[:, p * LANES:(p + 1) * LANES] for p in range(8)], axis=0)
    zq = jnp.zeros_like(q2)
    for e, qh in enumerate((jnp.where(lo, q2, zq), jnp.where(lo, zq, q2))):
        _masked_softmax_pv([qh],
                           lambda off: kv_ref[0, pl.ds(off, CK_WIDE), 0:LANES],
                           lambda _, off, e=e: kv_ref[0, pl.ds(off, CK_WIDE), (1 + e) * LANES:(2 + e) * LANES],
                           lambda off: bias_ref[:, pl.ds(off, CK_WIDE)],
                           0, s0 // CK_WIDE + 1, s_ref, rmax_ref, acc_ref.at[pl.ds(e, 1)], ck=CK_WIDE)
    o2 = _norm_pair(acc_ref[0], acc_ref[1], lo)
    for p in range(8):
        o_ref[0, :, p * LANES:(p + 1) * LANES] = o2[p * TQ:(p + 1) * TQ].astype(BF16)


def _dsa_attention(q, kv, qi, ki, wi, S):
    B = q.shape[0]
    return pl.pallas_call(
        functools.partial(_dsa_attn_kernel, S=S),
        grid=(B, S // TQ),
        in_specs=[
            pl.BlockSpec((1, TQ, 1024), lambda b, i: (b, i, 0)),
            pl.BlockSpec((1, S, 3 * LANES), lambda b, i: (b, 0, 0)),
            pl.BlockSpec((1, TQ, 512), lambda b, i: (b, i, 0)),
            pl.BlockSpec((1, S, LANES), lambda b, i: (b, 0, 0)),
            pl.BlockSpec((1, IDX_HEADS, TQ), lambda b, i: (b, 0, i)),
        ],
        out_specs=pl.BlockSpec((1, TQ, 1024), lambda b, i: (b, i, 0)),
        out_shape=jax.ShapeDtypeStruct((B, S, 1024), BF16),
        scratch_shapes=[pltpu.VMEM((S, TQ), jnp.int32),
                        pltpu.VMEM((TQ, S), F32),
                        pltpu.VMEM((1, 8 * TQ, S), F32),
                        pltpu.VMEM((1, 8 * TQ, LANES), F32),
                        pltpu.VMEM((2, 8 * TQ, LANES), F32)],
        compiler_params=_cparams(("parallel", "parallel")),
        name="dsa_attention",
    )(q, kv, qi, ki, wi)


def _out_proj_kernel(a_ref, w_ref, x_ref, g_ref, o_ref):
    y = _dot(a_ref[...], w_ref[...])
    o_ref[...] = x_ref[...] + _rms(y, g_ref[...])


def _out_proj(a2d, w, x2d, g):
    T = x2d.shape[0]
    tm = 512
    return pl.pallas_call(
        _out_proj_kernel,
        grid=(T // tm,),
        in_specs=[
            pl.BlockSpec((tm, 1024), lambda i: (i, 0)),
            pl.BlockSpec((1024, D_MODEL), lambda i: (0, 0)),
            pl.BlockSpec((tm, D_MODEL), lambda i: (i, 0)),
            pl.BlockSpec((1, D_MODEL), lambda i: (0, 0)),
        ],
        out_specs=pl.BlockSpec((tm, D_MODEL), lambda i: (i, 0)),
        out_shape=jax.ShapeDtypeStruct((T, D_MODEL), F32),
        compiler_params=_cparams(("parallel",)),
        name="mixer_out_proj",
    )(a2d, w, x2d, g)


FFN_TM = 512
FFN_FC = 256
FFN_NF = D_FF // FFN_FC
FFN_HALO = 16
FFN_DOWN_GROUP = 4


def _ffn_kernel(x_ref, xh_ref, g2_ref, g3_ref, wup_ref, cw_ref, cb_ref, wd_ref, o_ref, hext, ubuf, act_ref,
                *, tiles_per_seq):
    i = pl.program_id(0)
    tm = FFN_TM
    g2 = g2_ref[...]
    x = x_ref[...]
    hext[FFN_HALO:, :] = _rms(x, g2).astype(BF16)
    hh = _rms(xh_ref[...], g2)
    hh = jnp.where(i % tiles_per_seq == 0, 0.0, hh)
    hext[0:FFN_HALO, :] = hh.astype(BF16)
    he = hext[...]

    def conv(u_ref, col):
        w = cw_ref[:, col:col + FFN_FC]
        return (w[2:3] * u_ref[FFN_HALO:FFN_HALO + tm, :]
                + w[1:2] * u_ref[FFN_HALO - 1:FFN_HALO - 1 + tm, :]
                + w[0:1] * u_ref[FFN_HALO - 2:FFN_HALO - 2 + tm, :]
                + cb_ref[:, col:col + FFN_FC])

    y = None
    for f in range(FFN_NF):
        cv = f * FFN_FC
        cg = D_FF + f * FFN_FC
        uv = ubuf.at[(2 * f) % 4]
        ug = ubuf.at[(2 * f + 1) % 4]
        uv[...] = _dot(he, wup_ref[:, cv:cv + FFN_FC])
        ug[...] = _dot(he, wup_ref[:, cg:cg + FFN_FC])
        act_ref[:, cv:cv + FFN_FC] = (_gelu_tanh(conv(ug, cg)) * conv(uv, cv)).astype(BF16)
        if (f + 1) % FFN_DOWN_GROUP == 0 or f == FFN_NF - 1:
            k0 = (f // FFN_DOWN_GROUP) * FFN_DOWN_GROUP * FFN_FC
            part = _dot(act_ref[:, k0:cv + FFN_FC], wd_ref[k0:cv + FFN_FC, :])
            y = part if y is None else y + part
    o_ref[...] = x + _rms(y, g3_ref[...])


def _ffn(x2d, g2, g3, wup, cw, cb, wd, S):
    T = x2d.shape[0]
    tm = FFN_TM
    hb = tm // FFN_HALO
    const = lambda i: (0, 0)
    return pl.pallas_call(
        functools.partial(_ffn_kernel, tiles_per_seq=S // tm),
        grid=(T // tm,),
        in_specs=[
            pl.BlockSpec((tm, D_MODEL), lambda i: (i, 0)),
            pl.BlockSpec((FFN_HALO, D_MODEL), lambda i: (jnp.maximum(i * hb - 1, 0), 0)),
            pl.BlockSpec((1, D_MODEL), const),
            pl.BlockSpec((1, D_MODEL), const),
            pl.BlockSpec((D_MODEL, 2 * D_FF), const),
            pl.BlockSpec((CONV_WIDTH, 2 * D_FF), const),
            pl.BlockSpec((1, 2 * D_FF), const),
            pl.BlockSpec((D_FF, D_MODEL), const),
        ],
        out_specs=pl.BlockSpec((tm, D_MODEL), lambda i: (i, 0)),
        out_shape=jax.ShapeDtypeStruct((T, D_MODEL), F32),
        scratch_shapes=[
            pltpu.VMEM((tm + FFN_HALO, D_MODEL), BF16),
            pltpu.VMEM((4, tm + FFN_HALO, FFN_FC), F32),
            pltpu.VMEM((tm, D_FF), BF16),
        ],
        compiler_params=_cparams(("parallel",)),
        name="conv_ffn",
    )(x2d, x2d, g2, g3, wup, cw, cb, wd)


def _rope_tables(pos):
    half = HEAD_DIM // 2
    inv = ROPE_THETA ** (-jnp.arange(half, dtype=F32) / half)
    ang = pos.astype(F32)[:, None] * inv[None, :]
    cos, sin = jnp.cos(ang), jnp.sin(ang)
    return jnp.tile(cos, (1, 4)), jnp.tile(jnp.concatenate([-sin, sin], axis=-1), (1, 2))


def _tables(S):
    ncp = S // CMP_STRIDE
    n_sel = S // SEL_BLOCK
    cos2, sin2 = _rope_tables(jnp.arange(S, dtype=jnp.int32))
    cosk, sink = _rope_tables(jnp.arange(ncp, dtype=jnp.int32) * CMP_STRIDE + (CMP_BLOCK - 1))
    cosc = jnp.stack([cosk, jnp.ones_like(cosk)])
    sinc = jnp.stack([sink, jnp.zeros_like(sink)])
    sj = np.arange(n_sel)[:, None]
    ci = np.arange(ncp)[None, :]
    agg_t = ((ci * CMP_STRIDE <= sj * SEL_BLOCK + SEL_BLOCK - 1)
             & (ci * CMP_STRIDE + CMP_BLOCK - 1 >= sj * SEL_BLOCK)
             & (ci < (S - CMP_BLOCK) // CMP_STRIDE + 1))
    agg_t = jnp.asarray(agg_t, dtype=BF16)
    emat = jnp.asarray(np.arange(LANES)[:, None] == (np.arange(S)[None, :] // SEL_BLOCK), dtype=BF16)
    return (cos2, sin2, cosc, sinc, agg_t, emat)


def _pad_cols(w, width):
    return jnp.pad(w, ((0, 0), (0, width - w.shape[1])))


def _nsa_mixer(x2d, g_pre, g_post, w_in, pe_k, w1_k, w2_k, pe_v, w1_v, w2_v, w_out, B, S, tabs):
    cos2, sin2, cosc, sinc, agg, emat = tabs
    ncp = S // CMP_STRIDE
    q, kv, kc, vc, gate = _nsa_in_proj(x2d, g_pre, _pad_cols(w_in, IN_W_PAD).astype(BF16), cos2, sin2, S)

    def block_diag2(w):
        z = jnp.zeros_like(w)
        return jnp.concatenate([jnp.concatenate([w, z], axis=-1), jnp.concatenate([z, w], axis=-1)], axis=-2)

    pe2 = jnp.stack([jnp.tile(pe_k, (1, 2)), jnp.tile(pe_v, (1, 2))])
    w1g = jnp.stack([block_diag2(w.reshape(CMP_BLOCK, HEAD_DIM, CMP_HIDDEN)) for w in (w1_k, w1_v)]).astype(BF16)
    w2g = jnp.stack([block_diag2(jnp.tile(w, (1, 2))) for w in (w2_k, w2_v)]).astype(BF16)
    cmp4 = _nsa_compress(kc.reshape(B, S, LANES), vc.reshape(B, S, LANES), pe2, w1g, w2g,
                         cosc[0], sinc[0])
    ones = jnp.ones((B, ncp, HEAD_DIM), BF16)
    cmpkv = jnp.concatenate(
        [piece for g in range(NSA_KV_GROUPS) for piece in (
            cmp4[:, :, g * LANES:(g + 1) * LANES],
            cmp4[:, :, (2 + g) * LANES:(2 + g) * LANES + HEAD_DIM], ones,
            ones, cmp4[:, :, (2 + g) * LANES:(2 + g) * LANES + HEAD_DIM])], axis=-1)
    o = _nsa_attention(q.reshape(B, S, 1024), kv.reshape(B, S, 12 * LANES), cmpkv,
                       gate.reshape(B, S, 256), agg, emat, S)
    return _out_proj(o.reshape(B * S, 1024), w_out.astype(BF16), x2d, g_post)


def _dsa_mixer(x2d, g_pre, g_post, w_in, kv_norm, w_uk, w_uv, w_out, B, S, tabs):
    cos2, sin2 = tabs[0], tabs[1]
    wukv = jnp.concatenate([w_uk, w_uv], axis=1).astype(BF16)
    q, kv, qi, ki, wi = _dsa_in_proj(x2d, g_pre, _pad_cols(w_in, IN_W_PAD).astype(BF16),
                                     kv_norm.reshape(1, -1), wukv, cos2, sin2, S)
    o = _dsa_attention(q.reshape(B, S, 1024), kv.reshape(B, S, 3 * LANES), qi.reshape(B, S, 512),
                       ki.reshape(B, S, LANES), wi, S)
    return _out_proj(o.reshape(B * S, 1024), w_out.astype(BF16), x2d, g_post)


def kernel(x, norm_g, nsa_w_in, nsa_cmp_pe_k, nsa_cmp_w1_k, nsa_cmp_w2_k, nsa_cmp_pe_v, nsa_cmp_w1_v, nsa_cmp_w2_v, nsa_w_out, dsa_w_in, dsa_kv_norm, dsa_w_uk, dsa_w_uv, dsa_w_out, ffn_w_up, ffn_conv_w, ffn_conv_b, ffn_w_down):
    B, S, _ = x.shape
    tabs = _tables(S)
    x2d = x.reshape(B * S, D_MODEL)
    for i in range(DEPTH):
        g = norm_g[i].reshape(4, 1, D_MODEL)
        j = i // 2
        if i % 2 == 0:
            x2d = _nsa_mixer(x2d, g[0], g[1], nsa_w_in[j], nsa_cmp_pe_k[j], nsa_cmp_w1_k[j], nsa_cmp_w2_k[j],
                             nsa_cmp_pe_v[j], nsa_cmp_w1_v[j], nsa_cmp_w2_v[j], nsa_w_out[j], B, S, tabs)
        else:
            x2d = _dsa_mixer(x2d, g[0], g[1], dsa_w_in[j], dsa_kv_norm[j], dsa_w_uk[j], dsa_w_uv[j],
                             dsa_w_out[j], B, S, tabs)
        x2d = _ffn(x2d, g[2], g[3], ffn_w_up[i].astype(BF16), ffn_conv_w[i],
                   ffn_conv_b[i].reshape(1, -1), ffn_w_down[i].astype(BF16), S)
    return x2d.reshape(B, S, D_MODEL)
```

```python
import functools

import numpy as np
import jax
import jax.numpy as jnp
from jax import lax
from jax.experimental import pallas as pl
from jax.experimental.pallas import tpu as pltpu

F32 = jnp.float32
BF16 = jnp.bfloat16

D_MODEL = 1024
DEPTH = 4
N_HEADS = 16
HEAD_DIM = 64
ROPE_THETA = 10000.0
RMS_EPS = 1e-6
NEG_INF = -1e30
BIG = 1e30

NSA_KV_GROUPS = 2
CMP_BLOCK = 32
CMP_STRIDE = 16
CMP_HIDDEN = 256
SEL_BLOCK = 64
SEL_TOP_N = 16
WINDOW = 512

DSA_KV_RANK = 256
IDX_HEADS = 8
IDX_DIM = 64
DSA_TOP_K_MAX = 256

D_FF = 2816
CONV_WIDTH = 3

NSA_IN_W = N_HEADS * HEAD_DIM + 6 * NSA_KV_GROUPS * HEAD_DIM + 3 * N_HEADS
DSA_IN_W = N_HEADS * HEAD_DIM + DSA_KV_RANK + IDX_HEADS * IDX_DIM + IDX_DIM + IDX_HEADS

Q_SCALE = HEAD_DIM ** -0.5 * 1.4426950408889634

LANES = 128
IN_W_PAD = 1920
IN_PROJ_TM = 512
TQ = 256
CK = 256
CK_WIDE = 512
VMEM_LIMIT = 56 * 1024 * 1024


def _cparams(sem):
    return pltpu.CompilerParams(dimension_semantics=sem, vmem_limit_bytes=VMEM_LIMIT)


def _rms(x, g):
    return x * lax.rsqrt(jnp.mean(x * x, axis=-1, keepdims=True) + RMS_EPS) * g


def _lane_iota(shape):
    return lax.broadcasted_iota(jnp.int32, shape, len(shape) - 1)


def _rope128(x, cos2, sin2):
    lane = _lane_iota(x.shape)
    first = (lane % 64) < 32
    swapped = jnp.where(first, pltpu.roll(x, 96, axis=1), pltpu.roll(x, 32, axis=1))
    return x * cos2 + swapped * sin2


def _dup_halves(x):
    lo = _lane_iota(x.shape) < 64
    r = pltpu.roll(x, 64, axis=1)
    return jnp.where(lo, x, r), jnp.where(lo, r, x)


def _dot(a, b):
    return jnp.dot(a, b, preferred_element_type=F32)


def _dot_nt(a, b):
    return lax.dot_general(a, b, (((1,), (1,)), ((), ())), preferred_element_type=F32)


def _tree_sum(parts):
    while len(parts) > 1:
        parts = [parts[j] + parts[j + 1] if j + 1 < len(parts) else parts[j] for j in range(0, len(parts), 2)]
    return parts[0]


def _gelu_tanh(x):
    return 0.5 * x * (1.0 + jnp.tanh(0.7978845608028654 * (x + 0.044715 * (x * x * x))))


def _ones_halves(x):
    lo = _lane_iota(x.shape) < 64
    return jnp.where(lo, x, 1.0), jnp.where(lo, 1.0, x)


def _masked_softmax_pv(q_halves, k_of, v_of, bias_of, c_lo, c_hi, s_ref, rmax_ref, acc_ref, ck=CK):
    n_e = len(q_halves)
    rows = q_halves[0].shape[0]
    r = rows // TQ
    for e in range(n_e):
        rmax_ref[e] = jnp.full((rows, LANES), NEG_INF, F32)
        acc_ref[e] = jnp.zeros((rows, LANES), F32)

    def pass1(c, carry):
        off = pl.multiple_of(c * ck, ck)
        k = k_of(off)
        bias = bias_of(off)
        for e in range(n_e):
            for j in range(r):
                rs = slice(j * TQ, (j + 1) * TQ)
                s = _dot_nt(q_halves[e][rs], k) + bias
                s_ref[e, rs, pl.ds(off, ck)] = s
                cols = [s[:, i * LANES:(i + 1) * LANES] for i in range(ck // LANES)]
                rmax_ref[e, rs] = functools.reduce(jnp.maximum, cols, rmax_ref[e, rs])
        return carry

    lax.fori_loop(c_lo, c_hi, pass1, 0)
    for e in range(n_e):
        m = jnp.max(rmax_ref[e], axis=-1, keepdims=True)
        m = jnp.where(m > NEG_INF, m, 0.0)
        rmax_ref[e] = jnp.broadcast_to(m, (rows, LANES))

    def pass2(c, carry):
        off = pl.multiple_of(c * ck, ck)
        for e in range(n_e):
            v = v_of(e, off)
            for j in range(r):
                rs = slice(j * TQ, (j + 1) * TQ)
                mb = rmax_ref[e, rs]
                p = jnp.exp2(s_ref[e, rs, pl.ds(off, ck)] - jnp.concatenate([mb] * (ck // LANES), axis=1))
                acc_ref[e, rs] += _dot(p.astype(BF16), v)
        return carry

    lax.fori_loop(c_lo, c_hi, pass2, 0)


def _norm_pair(acc_a, acc_b, lo):
    num = jnp.where(lo, acc_a, acc_b)
    den = pltpu.roll(jnp.where(lo, acc_b, acc_a), 64, axis=1)
    return num / jnp.where(den > 0.0, den, 1.0)


def _nsa_in_kernel(x_ref, g_ref, w_ref, cos_ref, sin_ref, q_ref, kv_ref, kc_ref, vc_ref, gate_ref):
    h = _rms(x_ref[...], g_ref[...]).astype(BF16)
    proj = _dot(h, w_ref[...])
    cos2 = cos_ref[...]
    sin2 = sin_ref[...]
    for m in range(8):
        slab = proj[:, m * LANES:(m + 1) * LANES]
        q_ref[:, m * LANES:(m + 1) * LANES] = (_rope128(slab, cos2, sin2) * Q_SCALE).astype(BF16)
    kc_ref[...] = proj[:, 1024:1152]
    vc_ref[...] = proj[:, 1152:1280]
    ks = _rope128(proj[:, 1280:1408], cos2, sin2)
    vs = proj[:, 1408:1536]
    kw = _rope128(proj[:, 1536:1664], cos2, sin2)
    vw = proj[:, 1664:1792]
    for j, (kk, vv) in enumerate(((ks, vs), (kw, vw))):
        k0, k1 = _dup_halves(kk)
        v0, v1 = _dup_halves(vv)
        for g, (kd, vd) in enumerate(((k0, v0), (k1, v1))):
            base = (g * 6 + j * 3) * LANES
            v_lo, v_hi = _ones_halves(vd)
            kv_ref[:, base:base + LANES] = kd.astype(BF16)
            kv_ref[:, base + LANES:base + 2 * LANES] = v_lo.astype(BF16)
            kv_ref[:, base + 2 * LANES:base + 3 * LANES] = v_hi.astype(BF16)
    gt = proj[:, 1792:1920]
    sg = 1.0 / (1.0 + jnp.exp(-gt))
    gate_ref[:, 0:LANES] = sg
    gate_ref[:, LANES:2 * LANES] = pltpu.roll(sg, LANES - 24, axis=1)


def _nsa_in_proj(x2d, g, w, cos2, sin2, S):
    T = x2d.shape[0]
    tm = IN_PROJ_TM
    tps = S // tm
    return pl.pallas_call(
        _nsa_in_kernel,
        grid=(T // tm,),
        in_specs=[
            pl.BlockSpec((tm, D_MODEL), lambda i: (i, 0)),
            pl.BlockSpec((1, D_MODEL), lambda i: (0, 0)),
            pl.BlockSpec((D_MODEL, IN_W_PAD), lambda i: (0, 0)),
            pl.BlockSpec((tm, LANES), lambda i: (i % tps, 0)),
            pl.BlockSpec((tm, LANES), lambda i: (i % tps, 0)),
        ],
        out_specs=[
            pl.BlockSpec((tm, 1024), lambda i: (i, 0)),
            pl.BlockSpec((tm, 12 * LANES), lambda i: (i, 0)),
            pl.BlockSpec((tm, LANES), lambda i: (i, 0)),
            pl.BlockSpec((tm, LANES), lambda i: (i, 0)),
            pl.BlockSpec((tm, 256), lambda i: (i, 0)),
        ],
        out_shape=[
            jax.ShapeDtypeStruct((T, 1024), BF16),
            jax.ShapeDtypeStruct((T, 12 * LANES), BF16),
            jax.ShapeDtypeStruct((T, LANES), F32),
            jax.ShapeDtypeStruct((T, LANES), F32),
            jax.ShapeDtypeStruct((T, 256), F32),
        ],
        compiler_params=_cparams(("parallel",)),
        name="nsa_in_proj",
    )(x2d, g, w, cos2, sin2)


def _nsa_cmp_kernel(kc_ref, vc_ref, pe_ref, w1_ref, w2_ref, cos_ref, sin_ref, o_ref):
    ncp = o_ref.shape[1]
    for kv, src in enumerate((kc_ref, vc_ref)):
        parts = []
        for l in range(CMP_STRIDE):
            xl = src[0, pl.ds(l, ncp, stride=CMP_STRIDE), :]
            xn = pltpu.roll(xl, ncp - 1, axis=0)
            parts.append(_dot((xl + pe_ref[kv, l:l + 1, :]).astype(BF16), w1_ref[kv, l]))
            ln = CMP_STRIDE + l
            parts.append(_dot((xn + pe_ref[kv, ln:ln + 1, :]).astype(BF16), w1_ref[kv, ln]))
        hid = _gelu_tanh(_tree_sum(parts))
        out = _dot(hid.astype(BF16), w2_ref[kv])
        for g in range(NSA_KV_GROUPS):
            slab = out[:, g * LANES:(g + 1) * LANES]
            if kv == 0:
                slab = _rope128(slab, cos_ref[...], sin_ref[...])
            j = kv * NSA_KV_GROUPS + g
            o_ref[0, :, j * LANES:(j + 1) * LANES] = slab.astype(BF16)


def _nsa_compress(kc, vc, pe2, w1g, w2g, cosk, sink):
    B, S, _ = kc.shape
    ncp = S // CMP_STRIDE
    full = lambda *shape: pl.BlockSpec(shape, lambda b: (0,) * len(shape))
    return pl.pallas_call(
        _nsa_cmp_kernel,
        grid=(B,),
        in_specs=[
            pl.BlockSpec((1, S, LANES), lambda b: (b, 0, 0)),
            pl.BlockSpec((1, S, LANES), lambda b: (b, 0, 0)),
            full(2, CMP_BLOCK, LANES),
            full(2, CMP_BLOCK, LANES, 2 * CMP_HIDDEN),
            full(2, 2 * CMP_HIDDEN, 2 * LANES),
            full(ncp, LANES),
            full(ncp, LANES),
        ],
        out_specs=pl.BlockSpec((1, ncp, 4 * LANES), lambda b: (b, 0, 0)),
        out_shape=jax.ShapeDtypeStruct((B, ncp, 4 * LANES), BF16),
        compiler_params=_cparams(("parallel",)),
        name="nsa_compress",
    )(kc, vc, pe2, w1g, w2g, cosk, sink)


def _nsa_attn_kernel(q_ref, kv_ref, cmp_ref, gate_ref, agg_ref, e_ref, o_ref,
                     s_ref, rmax_ref, acc_ref, *, S):
    n_sel = S // SEL_BLOCK
    top_n = min(SEL_TOP_N, n_sel)
    i = pl.program_id(2)
    s0 = i * TQ
    lo = _lane_iota((1, LANES)) < 64
    q = q_ref[0]
    q2 = jnp.concatenate([q[:, p * LANES:(p + 1) * LANES] for p in range(4)], axis=0)
    zero = jnp.zeros_like(q2)
    q_lo = jnp.where(lo, q2, zero)
    q_hi = jnp.where(lo, zero, q2)
    t = s0 + lax.broadcasted_iota(jnp.int32, (TQ, 1), 0)

    kc = cmp_ref[0, :, 0:LANES]
    ncp = kc.shape[0]
    cend = _lane_iota((1, ncp)) * CMP_STRIDE + (CMP_BLOCK - 1)
    mask_c = (cend <= t)[None]

    def cmp_branch(qh, vc1):
        s3 = jnp.where(mask_c, _dot_nt(qh, kc).reshape(4, TQ, ncp), NEG_INF)
        e = jnp.where(mask_c, jnp.exp2(s3 - jnp.max(s3, axis=-1, keepdims=True)), 0.0)
        e = e.reshape(4 * TQ, ncp)
        return e, _dot(e.astype(BF16), vc1)

    ea, acc_ca = cmp_branch(q_lo, cmp_ref[0, :, LANES:2 * LANES])
    eb, acc_cb = cmp_branch(q_hi, cmp_ref[0, :, 2 * LANES:3 * LANES])
    oc2 = _norm_pair(acc_ca, acc_cb, lo)
    den = pltpu.roll(jnp.where(lo, acc_cb, acc_ca), 64, axis=1)
    l_a = jnp.where(lo, den, acc_ca)
    l_b = jnp.where(lo, acc_cb, den)
    pa = ea / jnp.where(l_a > 0.0, l_a, 1.0)
    pb = eb / jnp.where(l_b > 0.0, l_b, 1.0)
    psum = _tree_sum([pa[p * TQ:(p + 1) * TQ] for p in range(4)] + [pb[p * TQ:(p + 1) * TQ] for p in range(4)])
    agg = agg_ref[...]
    p_hi = psum.astype(BF16)
    r1 = psum - p_hi.astype(F32)
    p_mid = r1.astype(BF16)
    p_lo = (r1 - p_mid.astype(F32)).astype(BF16)
    imp = _dot_nt(agg, p_hi) + _dot_nt(agg, p_mid) + _dot_nt(agg, p_lo)

    tq = s0 + _lane_iota((1, TQ))
    jblk = lax.broadcasted_iota(jnp.int32, (n_sel, 1), 0)
    cur = jnp.right_shift(tq, SEL_BLOCK.bit_length() - 1)
    forced = (jblk == 0) | (jblk == cur) | (jblk == cur - 1)
    causal_blk = jblk * SEL_BLOCK <= tq
    impm = jnp.where(causal_blk, jnp.where(forced, BIG, imp), NEG_INF)
    beats = []
    for k in range(n_sel):
        row = impm[k:k + 1, :]
        tie_k = jnp.where(jblk > k, 1.0, 0.0)
        beats.append(jnp.where(row > impm, 1.0, jnp.where(row == impm, tie_k, 0.0)))
    rank = _tree_sum(beats)
    sel_t = jnp.where(rank < float(top_n), 1.0, 0.0)
    sel = jnp.concatenate([sel_t, jnp.zeros((LANES - n_sel, TQ), F32)], axis=0).T.astype(BF16)
    q_halves = [q_lo, q_hi]
    c_hi = s0 // CK + 1

    def sel_bias(off):
        kpos = off + _lane_iota((1, CK_WIDE))
        picked = _dot(sel, e_ref[:, pl.ds(off, CK_WIDE)])
        return jnp.where(picked > 0.5, jnp.where(kpos <= t, 0.0, NEG_INF), NEG_INF)

    _masked_softmax_pv(q_halves,
                       lambda off: kv_ref[0, pl.ds(off, CK_WIDE), 0:LANES],
                       lambda e, off: kv_ref[0, pl.ds(off, CK_WIDE), (1 + e) * LANES:(2 + e) * LANES],
                       sel_bias, 0, s0 // CK_WIDE + 1, s_ref, rmax_ref, acc_ref, ck=CK_WIDE)
    os2 = _norm_pair(acc_ref[0], acc_ref[1], lo)

    def win_bias(off):
        dist = t - (off + _lane_iota((1, CK)))
        return jnp.where(dist >= 0, jnp.where(dist < WINDOW, 0.0, NEG_INF), NEG_INF)

    c_lo = jnp.maximum(s0 - (WINDOW - 1), 0) // CK
    _masked_softmax_pv(q_halves,
                       lambda off: kv_ref[0, pl.ds(off, CK), 3 * LANES:4 * LANES],
                       lambda e, off: kv_ref[0, pl.ds(off, CK), (4 + e) * LANES:(5 + e) * LANES],
                       win_bias, c_lo, c_hi, s_ref, rmax_ref, acc_ref)
    ow2 = _norm_pair(acc_ref[0], acc_ref[1], lo)

    gt = gate_ref[0]
    for p in range(4):
        rs = slice(p * TQ, (p + 1) * TQ)
        ca, cb = (2 * p) * 3, (2 * p + 1) * 3
        g_c = jnp.where(lo, gt[:, ca:ca + 1], gt[:, cb:cb + 1])
        g_s = jnp.where(lo, gt[:, ca + 1:ca + 2], gt[:, cb + 1:cb + 2])
        g_w = jnp.where(lo, gt[:, ca + 2:ca + 3], gt[:, cb + 2:cb + 3])
        o_ref[0, :, p * LANES:(p + 1) * LANES] = (g_c * oc2[rs] + g_s * os2[rs] + g_w * ow2[rs]).astype(BF16)


def _nsa_attention(q, kv, cmpkv, gate, agg, emat, S):
    B = q.shape[0]
    ncp = cmpkv.shape[1]
    return pl.pallas_call(
        functools.partial(_nsa_attn_kernel, S=S),
        grid=(B, NSA_KV_GROUPS, S // TQ),
        in_specs=[
            pl.BlockSpec((1, TQ, 512), lambda b, g, i: (b, i, g)),
            pl.BlockSpec((1, S, 6 * LANES), lambda b, g, i: (b, 0, g)),
            pl.BlockSpec((1, ncp, 3 * LANES), lambda b, g, i: (b, 0, g)),
            pl.BlockSpec((1, TQ, LANES), lambda b, g, i: (b, i, g)),
            pl.BlockSpec((S // SEL_BLOCK, ncp), lambda b, g, i: (0, 0)),
            pl.BlockSpec((LANES, S), lambda b, g, i: (0, 0)),
        ],
        out_specs=pl.BlockSpec((1, TQ, 512), lambda b, g, i: (b, i, g)),
        out_shape=jax.ShapeDtypeStruct((B, S, 1024), BF16),
        scratch_shapes=[pltpu.VMEM((2, 4 * TQ, S), F32),
                        pltpu.VMEM((2, 4 * TQ, LANES), F32),
                        pltpu.VMEM((2, 4 * TQ, LANES), F32)],
        compiler_params=_cparams(("parallel", "parallel", "parallel")),
        name="nsa_attention",
    )(q, kv, cmpkv, gate, agg, emat)


def _dsa_in_kernel(x_ref, g_ref, w_ref, kvn_ref, wukv_ref, cos_ref, sin_ref,
                   q_ref, kv_ref, qi_ref, ki_ref, wi_ref):
    h = _rms(x_ref[...], g_ref[...]).astype(BF16)
    proj = _dot(h, w_ref[...])
    cos2 = cos_ref[...]
    sin2 = sin_ref[...]
    for m in range(8):
        slab = proj[:, m * LANES:(m + 1) * LANES]
        q_ref[:, m * LANES:(m + 1) * LANES] = (_rope128(slab, cos2, sin2) * Q_SCALE).astype(BF16)
    c = _rms(proj[:, 1024:1280], kvn_ref[...]).astype(BF16)
    kvu = _dot(c, wukv_ref[...])
    kd, vd = _dup_halves(kvu)
    v_lo, v_hi = _ones_halves(vd)
    kv_ref[:, 0:LANES] = _rope128(kd, cos2, sin2).astype(BF16)
    kv_ref[:, LANES:2 * LANES] = v_lo.astype(BF16)
    kv_ref[:, 2 * LANES:3 * LANES] = v_hi.astype(BF16)
    for m in range(4):
        slab = proj[:, 1280 + m * LANES:1280 + (m + 1) * LANES]
        qi_ref[:, m * LANES:(m + 1) * LANES] = (_rope128(slab, cos2, sin2) * (IDX_DIM ** -0.5)).astype(BF16)
    last = proj[:, 1792:1920]
    kid, wid = _dup_halves(last)
    ki_ref[...] = _rope128(kid, cos2, sin2).astype(BF16)
    wi_ref[0] = (wid * (IDX_HEADS ** -0.5)).T[0:IDX_HEADS, :]


def _dsa_in_proj(x2d, g, w, kvn, wukv, cos2, sin2, S):
    T = x2d.shape[0]
    tm = IN_PROJ_TM
    tps = S // tm
    return pl.pallas_call(
        _dsa_in_kernel,
        grid=(T // tm,),
        in_specs=[
            pl.BlockSpec((tm, D_MODEL), lambda i: (i, 0)),
            pl.BlockSpec((1, D_MODEL), lambda i: (0, 0)),
            pl.BlockSpec((D_MODEL, IN_W_PAD), lambda i: (0, 0)),
            pl.BlockSpec((1, DSA_KV_RANK), lambda i: (0, 0)),
            pl.BlockSpec((DSA_KV_RANK, LANES), lambda i: (0, 0)),
            pl.BlockSpec((tm, LANES), lambda i: (i % tps, 0)),
            pl.BlockSpec((tm, LANES), lambda i: (i % tps, 0)),
        ],
        out_specs=[
            pl.BlockSpec((tm, 1024), lambda i: (i, 0)),
            pl.BlockSpec((tm, 3 * LANES), lambda i: (i, 0)),
            pl.BlockSpec((tm, 512), lambda i: (i, 0)),
            pl.BlockSpec((tm, LANES), lambda i: (i, 0)),
            pl.BlockSpec((1, IDX_HEADS, tm), lambda i: (i // tps, 0, i % tps)),
        ],
        out_shape=[
            jax.ShapeDtypeStruct((T, 1024), BF16),
            jax.ShapeDtypeStruct((T, 3 * LANES), BF16),
            jax.ShapeDtypeStruct((T, 512), BF16),
            jax.ShapeDtypeStruct((T, LANES), BF16),
            jax.ShapeDtypeStruct((T // S, IDX_HEADS, S), F32),
        ],
        compiler_params=_cparams(("parallel",)),
        name="dsa_in_proj",
    )(x2d, g, w, kvn, wukv, cos2, sin2)


def _sortable(v):
    bits = lax.bitcast_convert_type(v + 0.0, jnp.int32)
    return bits ^ ((bits >> 31) & jnp.int32(0x7FFFFFFF))


_NEG_BITS = np.array([NEG_INF], np.float32).view(np.int32)
_NEG_KEY = int((_NEG_BITS ^ ((_NEG_BITS >> 31) & np.int32(0x7FFFFFFF)))[0])
_INT_MIN = -(2 ** 31)


def _dsa_attn_kernel(q_ref, kv_ref, qi_ref, ki_ref, wi_ref, o_ref,
                     key_ref, bias_ref, s_ref, rmax_ref, acc_ref, *, S):
    top_k = min(DSA_TOP_K_MAX, S // 4)
    idx_bits = (S - 1).bit_length()
    i = pl.program_id(1)
    s0 = i * TQ
    n_chunks = s0 // CK + 1
    lo = _lane_iota((1, LANES)) < 64
    tq = s0 + _lane_iota((1, TQ))

    qi = qi_ref[0]
    qi2 = jnp.concatenate([qi[:, p * LANES:(p + 1) * LANES] for p in range(4)], axis=0)
    zi = jnp.zeros_like(qi2)
    qi_lo = jnp.where(lo, qi2, zi)
    qi_hi = jnp.where(lo, zi, qi2)
    wt = wi_ref[0]

    def idx_body(c, carry):
        off = pl.multiple_of(c * CK, CK)
        ki = ki_ref[0, pl.ds(off, CK), :]
        sa = _dot_nt(ki, qi_lo)
        sb = _dot_nt(ki, qi_hi)
        score = jnp.zeros((CK, TQ), F32)
        for p in range(4):
            cs = slice(p * TQ, (p + 1) * TQ)
            score = score + jnp.maximum(sa[:, cs], 0.0) * wt[2 * p:2 * p + 1, :]
            score = score + jnp.maximum(sb[:, cs], 0.0) * wt[2 * p + 1:2 * p + 2, :]
        kpos = off + lax.broadcasted_iota(jnp.int32, (CK, 1), 0)
        score = jnp.where(kpos <= tq, score, NEG_INF)
        key_ref[pl.ds(off, CK), :] = _sortable(score)
        return carry

    lax.fori_loop(0, n_chunks, idx_body, 0)

    n_out = (S - n_chunks * CK).astype(F32)

    def count(pred_fn):
        def body(c, acc):
            off = pl.multiple_of(c * CK, CK)
            hit = jnp.where(pred_fn(key_ref[pl.ds(off, CK), :], off), 1.0, 0.0)
            return acc + _tree_sum([hit[8 * j:8 * j + 8] for j in range(CK // 8)])
        acc = lax.fori_loop(0, n_chunks, body, jnp.zeros((8, TQ), F32))
        return jnp.sum(acc, axis=0, keepdims=True)

    def bit_body(it, tu):
        bit = jnp.left_shift(jnp.int32(1), 31 - it)
        cand = (tu | bit) ^ jnp.int32(_INT_MIN)
        cnt = count(lambda kk, off: kk >= cand)
        cnt = cnt + n_out * jnp.where(jnp.int32(_NEG_KEY) >= cand, 1.0, 0.0)
        return jnp.where(cnt >= float(top_k), tu | bit, tu)

    tu = lax.fori_loop(0, 32, bit_body, jnp.zeros((1, TQ), jnp.int32))
    thr = tu ^ jnp.int32(_INT_MIN)
    n_gt = count(lambda kk, off: kk > thr)
    n_gt = n_gt + n_out * jnp.where(jnp.int32(_NEG_KEY) > thr, 1.0, 0.0)
    need = float(top_k) - n_gt
    n_eq = count(lambda kk, off: kk == thr)

    def row_iota(off):
        return off + lax.broadcasted_iota(jnp.int32, (CK, 1), 0)

    def tie_search():
        def tie_body(it, jb):
            bit = jnp.left_shift(jnp.int32(1), idx_bits - 1 - it)
            cand = jb | bit
            cnt = count(lambda kk, off: (kk == thr) & (row_iota(off) < cand))
            return jnp.where(cnt < need, cand, jb)
        return lax.fori_loop(0, idx_bits, tie_body, jnp.zeros((1, TQ), jnp.int32))

    jb = lax.cond(jnp.max(n_eq - need) > 0.0, tie_search,
                  lambda: jnp.full((1, TQ), 2 ** idx_bits - 1, jnp.int32))

    eye = jnp.where(lax.broadcasted_iota(jnp.int32, (TQ, TQ), 0) == _lane_iota((TQ, TQ)), 1.0, 0.0).astype(BF16)

    def bias_body(c, carry):
        off = pl.multiple_of(c * CK, CK)
        kk = key_ref[pl.ds(off, CK), :]
        kpos = row_iota(off)
        tie = jnp.where(kk == thr, jnp.where(kpos <= jb, 1.0, 0.0), 0.0)
        picked = jnp.where(kpos <= tq, jnp.where(kk > thr, 1.0, tie), 0.0)
        sel_qk = _dot_nt(eye, picked.astype(BF16))
        bias_ref[:, pl.ds(off, CK)] = (sel_qk - 1.0) * BIG
        return carry

    lax.fori_loop(0, n_chunks, bias_body, 0)

    @pl.when(n_chunks * CK < (s0 // CK_WIDE + 1) * CK_WIDE)
    def _():
        bias_ref[:, pl.ds(pl.multiple_of(n_chunks * CK, CK), CK)] = jnp.full((TQ, CK), NEG_INF, F32)

    q = q_ref[0]
    q2 = jnp.concatenate([q[:, p * LANES:(p + 1) * LANES] for p in range(8)], axis=0)
    zq = jnp.zeros_like(q2)
    for e, qh in enumerate((jnp.where(lo, q2, zq), jnp.where(lo, zq, q2))):
        _masked_softmax_pv([qh],
                           lambda off: kv_ref[0, pl.ds(off, CK_WIDE), 0:LANES],
                           lambda _, off, e=e: kv_ref[0, pl.ds(off, CK_WIDE), (1 + e) * LANES:(2 + e) * LANES],
                           lambda off: bias_ref[:, pl.ds(off, CK_WIDE)],
                           0, s0 // CK_WIDE + 1, s_ref, rmax_ref, acc_ref.at[pl.ds(e, 1)], ck=CK_WIDE)
    o2 = _norm_pair(acc_ref[0], acc_ref[1], lo)
    for p in range(8):
        o_ref[0, :, p * LANES:(p + 1) * LANES] = o2[p * TQ:(p + 1) * TQ].astype(BF16)


def _dsa_attention(q, kv, qi, ki, wi, S):
    B = q.shape[0]
    return pl.pallas_call(
        functools.partial(_dsa_attn_kernel, S=S),
        grid=(B, S // TQ),
        in_specs=[
            pl.BlockSpec((1, TQ, 1024), lambda b, i: (b, i, 0)),
            pl.BlockSpec((1, S, 3 * LANES), lambda b, i: (b, 0, 0)),
            pl.BlockSpec((1, TQ, 512), lambda b, i: (b, i, 0)),
            pl.BlockSpec((1, S, LANES), lambda b, i: (b, 0, 0)),
            pl.BlockSpec((1, IDX_HEADS, TQ), lambda b, i: (b, 0, i)),
        ],
        out_specs=pl.BlockSpec((1, TQ, 1024), lambda b, i: (b, i, 0)),
        out_shape=jax.ShapeDtypeStruct((B, S, 1024), BF16),
        scratch_shapes=[pltpu.VMEM((S, TQ), jnp.int32),
                        pltpu.VMEM((TQ, S), F32),
                        pltpu.VMEM((1, 8 * TQ, S), F32),
                        pltpu.VMEM((1, 8 * TQ, LANES), F32),
                        pltpu.VMEM((2, 8 * TQ, LANES), F32)],
        compiler_params=_cparams(("parallel", "parallel")),
        name="dsa_attention",
    )(q, kv, qi, ki, wi)


def _out_proj_kernel(a_ref, w_ref, x_ref, g_ref, o_ref):
    y = _dot(a_ref[...], w_ref[...])
    o_ref[...] = x_ref[...] + _rms(y, g_ref[...])


def _out_proj(a2d, w, x2d, g):
    T = x2d.shape[0]
    tm = 512
    return pl.pallas_call(
        _out_proj_kernel,
        grid=(T // tm,),
        in_specs=[
            pl.BlockSpec((tm, 1024), lambda i: (i, 0)),
            pl.BlockSpec((1024, D_MODEL), lambda i: (0, 0)),
            pl.BlockSpec((tm, D_MODEL), lambda i: (i, 0)),
            pl.BlockSpec((1, D_MODEL), lambda i: (0, 0)),
        ],
        out_specs=pl.BlockSpec((tm, D_MODEL), lambda i: (i, 0)),
        out_shape=jax.ShapeDtypeStruct((T, D_MODEL), F32),
        compiler_params=_cparams(("parallel",)),
        name="mixer_out_proj",
    )(a2d, w, x2d, g)


FFN_TM = 512
FFN_FC = 256
FFN_NF = D_FF // FFN_FC
FFN_HALO = 16
FFN_DOWN_GROUP = 11
FFN_UBUFS = 4


def _ffn_kernel(x_ref, xh_ref, g2_ref, g3_ref, wup_ref, cw_ref, cb_ref, wd_ref, o_ref, hext, ubuf, act_ref,
                *, tiles_per_seq):
    i = pl.program_id(0)
    tm = FFN_TM
    g2 = g2_ref[...]
    x = x_ref[...]
    hext[FFN_HALO:, :] = _rms(x, g2).astype(BF16)
    hh = _rms(xh_ref[...], g2)
    hh = jnp.where(i % tiles_per_seq == 0, 0.0, hh)
    hext[0:FFN_HALO, :] = hh.astype(BF16)
    he = hext[...]

    def conv(u_ref, col):
        w = cw_ref[:, col:col + FFN_FC]
        return (w[2:3] * u_ref[FFN_HALO:FFN_HALO + tm, :]
                + w[1:2] * u_ref[FFN_HALO - 1:FFN_HALO - 1 + tm, :]
                + w[0:1] * u_ref[FFN_HALO - 2:FFN_HALO - 2 + tm, :]
                + cb_ref[:, col:col + FFN_FC])

    y = None
    for f in range(FFN_NF):
        cv = f * FFN_FC
        cg = D_FF + f * FFN_FC
        uv = ubuf.at[(2 * f) % FFN_UBUFS]
        ug = ubuf.at[(2 * f + 1) % FFN_UBUFS]
        uv[...] = _dot(he, wup_ref[:, cv:cv + FFN_FC])
        ug[...] = _dot(he, wup_ref[:, cg:cg + FFN_FC])
        act_ref[:, cv:cv + FFN_FC] = (_gelu_tanh(conv(ug, cg)) * conv(uv, cv)).astype(BF16)
        if (f + 1) % FFN_DOWN_GROUP == 0 or f == FFN_NF - 1:
            k0 = (f // FFN_DOWN_GROUP) * FFN_DOWN_GROUP * FFN_FC
            part = _dot(act_ref[:, k0:cv + FFN_FC], wd_ref[k0:cv + FFN_FC, :])
            y = part if y is None else y + part
    o_ref[...] = x + _rms(y, g3_ref[...])


def _ffn(x2d, g2, g3, wup, cw, cb, wd, S):
    T = x2d.shape[0]
    tm = FFN_TM
    hb = tm // FFN_HALO
    const = lambda i: (0, 0)
    return pl.pallas_call(
        functools.partial(_ffn_kernel, tiles_per_seq=S // tm),
        grid=(T // tm,),
        in_specs=[
            pl.BlockSpec((tm, D_MODEL), lambda i: (i, 0)),
            pl.BlockSpec((FFN_HALO, D_MODEL), lambda i: (jnp.maximum(i * hb - 1, 0), 0)),
            pl.BlockSpec((1, D_MODEL), const),
            pl.BlockSpec((1, D_MODEL), const),
            pl.BlockSpec((D_MODEL, 2 * D_FF), const),
            pl.BlockSpec((CONV_WIDTH, 2 * D_FF), const),
            pl.BlockSpec((1, 2 * D_FF), const),
            pl.BlockSpec((D_FF, D_MODEL), const),
        ],
        out_specs=pl.BlockSpec((tm, D_MODEL), lambda i: (i, 0)),
        out_shape=jax.ShapeDtypeStruct((T, D_MODEL), F32),
        scratch_shapes=[
            pltpu.VMEM((tm + FFN_HALO, D_MODEL), BF16),
            pltpu.VMEM((FFN_UBUFS, tm + FFN_HALO, FFN_FC), F32),
            pltpu.VMEM((tm, D_FF), BF16),
        ],
        compiler_params=_cparams(("parallel",)),
        name="conv_ffn",
    )(x2d, x2d, g2, g3, wup, cw, cb, wd)


def _rope_tables(pos):
    half = HEAD_DIM // 2
    inv = ROPE_THETA ** (-jnp.arange(half, dtype=F32) / half)
    ang = pos.astype(F32)[:, None] * inv[None, :]
    cos, sin = jnp.cos(ang), jnp.sin(ang)
    return jnp.tile(cos, (1, 4)), jnp.tile(jnp.concatenate([-sin, sin], axis=-1), (1, 2))


def _tables(S):
    ncp = S // CMP_STRIDE
    n_sel = S // SEL_BLOCK
    cos2, sin2 = _rope_tables(jnp.arange(S, dtype=jnp.int32))
    cosk, sink = _rope_tables(jnp.arange(ncp, dtype=jnp.int32) * CMP_STRIDE + (CMP_BLOCK - 1))
    cosc = jnp.stack([cosk, jnp.ones_like(cosk)])
    sinc = jnp.stack([sink, jnp.zeros_like(sink)])
    sj = np.arange(n_sel)[:, None]
    ci = np.arange(ncp)[None, :]
    agg_t = ((ci * CMP_STRIDE <= sj * SEL_BLOCK + SEL_BLOCK - 1)
             & (ci * CMP_STRIDE + CMP_BLOCK - 1 >= sj * SEL_BLOCK)
             & (ci < (S - CMP_BLOCK) // CMP_STRIDE + 1))
    agg_t = jnp.asarray(agg_t, dtype=BF16)
    emat = jnp.asarray(np.arange(LANES)[:, None] == (np.arange(S)[None, :] // SEL_BLOCK), dtype=BF16)
    return (cos2, sin2, cosc, sinc, agg_t, emat)


def _pad_cols(w, width):
    return jnp.pad(w, ((0, 0), (0, width - w.shape[1])))


def _nsa_mixer(x2d, g_pre, g_post, w_in, pe_k, w1_k, w2_k, pe_v, w1_v, w2_v, w_out, B, S, tabs):
    cos2, sin2, cosc, sinc, agg, emat = tabs
    ncp = S // CMP_STRIDE
    q, kv, kc, vc, gate = _nsa_in_proj(x2d, g_pre, _pad_cols(w_in, IN_W_PAD).astype(BF16), cos2, sin2, S)

    def block_diag2(w):
        z = jnp.zeros_like(w)
        return jnp.concatenate([jnp.concatenate([w, z], axis=-1), jnp.concatenate([z, w], axis=-1)], axis=-2)

    pe2 = jnp.stack([jnp.tile(pe_k, (1, 2)), jnp.tile(pe_v, (1, 2))])
    w1g = jnp.stack([block_diag2(w.reshape(CMP_BLOCK, HEAD_DIM, CMP_HIDDEN)) for w in (w1_k, w1_v)]).astype(BF16)
    w2g = jnp.stack([block_diag2(jnp.tile(w, (1, 2))) for w in (w2_k, w2_v)]).astype(BF16)
    cmp4 = _nsa_compress(kc.reshape(B, S, LANES), vc.reshape(B, S, LANES), pe2, w1g, w2g,
                         cosc[0], sinc[0])
    ones = jnp.ones((B, ncp, HEAD_DIM), BF16)
    cmpkv = jnp.concatenate(
        [piece for g in range(NSA_KV_GROUPS) for piece in (
            cmp4[:, :, g * LANES:(g + 1) * LANES],
            cmp4[:, :, (2 + g) * LANES:(2 + g) * LANES + HEAD_DIM], ones,
            ones, cmp4[:, :, (2 + g) * LANES:(2 + g) * LANES + HEAD_DIM])], axis=-1)
    o = _nsa_attention(q.reshape(B, S, 1024), kv.reshape(B, S, 12 * LANES), cmpkv,
                       gate.reshape(B, S, 256), agg, emat, S)
    return _out_proj(o.reshape(B * S, 1024), w_out.astype(BF16), x2d, g_post)


def _dsa_mixer(x2d, g_pre, g_post, w_in, kv_norm, w_uk, w_uv, w_out, B, S, tabs):
    cos2, sin2 = tabs[0], tabs[1]
    wukv = jnp.concatenate([w_uk, w_uv], axis=1).astype(BF16)
    q, kv, qi, ki, wi = _dsa_in_proj(x2d, g_pre, _pad_cols(w_in, IN_W_PAD).astype(BF16),
                                     kv_norm.reshape(1, -1), wukv, cos2, sin2, S)
    o = _dsa_attention(q.reshape(B, S, 1024), kv.reshape(B, S, 3 * LANES), qi.reshape(B, S, 512),
                       ki.reshape(B, S, LANES), wi, S)
    return _out_proj(o.reshape(B * S, 1024), w_out.astype(BF16), x2d, g_post)


def kernel(x, norm_g, nsa_w_in, nsa_cmp_pe_k, nsa_cmp_w1_k, nsa_cmp_w2_k, nsa_cmp_pe_v, nsa_cmp_w1_v, nsa_cmp_w2_v, nsa_w_out, dsa_w_in, dsa_kv_norm, dsa_w_uk, dsa_w_uv, dsa_w_out, ffn_w_up, ffn_conv_w, ffn_conv_b, ffn_w_down):
    B, S, _ = x.shape
    tabs = _tables(S)
    x2d = x.reshape(B * S, D_MODEL)
    for i in range(DEPTH):
        g = norm_g[i].reshape(4, 1, D_MODEL)
        j = i // 2
        if i % 2 == 0:
            x2d = _nsa_mixer(x2d, g[0], g[1], nsa_w_in[j], nsa_cmp_pe_k[j], nsa_cmp_w1_k[j], nsa_cmp_w2_k[j],
                             nsa_cmp_pe_v[j], nsa_cmp_w1_v[j], nsa_cmp_w2_v[j], nsa_w_out[j], B, S, tabs)
        else:
            x2d = _dsa_mixer(x2d, g[0], g[1], dsa_w_in[j], dsa_kv_norm[j], dsa_w_uk[j], dsa_w_uv[j],
                             dsa_w_out[j], B, S, tabs)
        x2d = _ffn(x2d, g[2], g[3], ffn_w_up[i].astype(BF16), ffn_conv_w[i],
                   ffn_conv_b[i].reshape(1, -1), ffn_w_down[i].astype(BF16), S)
    return x2d.reshape(B, S, D_MODEL)
```

```python
import functools

import numpy as np
import jax
import jax.numpy as jnp
from jax import lax
from jax.experimental import pallas as pl
from jax.experimental.pallas import tpu as pltpu

F32 = jnp.float32
BF16 = jnp.bfloat16

D_MODEL = 1024
DEPTH = 4
N_HEADS = 16
HEAD_DIM = 64
ROPE_THETA = 10000.0
RMS_EPS = 1e-6
NEG_INF = -1e30
BIG = 1e30

NSA_KV_GROUPS = 2
CMP_BLOCK = 32
CMP_STRIDE = 16
CMP_HIDDEN = 256
SEL_BLOCK = 64
SEL_TOP_N = 16
WINDOW = 512

DSA_KV_RANK = 256
IDX_HEADS = 8
IDX_DIM = 64
DSA_TOP_K_MAX = 256

D_FF = 2816
CONV_WIDTH = 3

NSA_IN_W = N_HEADS * HEAD_DIM + 6 * NSA_KV_GROUPS * HEAD_DIM + 3 * N_HEADS
DSA_IN_W = N_HEADS * HEAD_DIM + DSA_KV_RANK + IDX_HEADS * IDX_DIM + IDX_DIM + IDX_HEADS

Q_SCALE = HEAD_DIM ** -0.5 * 1.4426950408889634

LANES = 128
IN_W_PAD = 1920
IN_PROJ_TM = 512
TQ = 256
CK = 256
CK_WIDE = 512
VMEM_LIMIT = 56 * 1024 * 1024


def _cparams(sem):
    return pltpu.CompilerParams(dimension_semantics=sem, vmem_limit_bytes=VMEM_LIMIT)


def _rms(x, g):
    return x * lax.rsqrt(jnp.mean(x * x, axis=-1, keepdims=True) + RMS_EPS) * g


def _lane_iota(shape):
    return lax.broadcasted_iota(jnp.int32, shape, len(shape) - 1)


def _rope128(x, cos2, sin2):
    lane = _lane_iota(x.shape)
    first = (lane % 64) < 32
    swapped = jnp.where(first, pltpu.roll(x, 96, axis=1), pltpu.roll(x, 32, axis=1))
    return x * cos2 + swapped * sin2


def _dup_halves(x):
    lo = _lane_iota(x.shape) < 64
    r = pltpu.roll(x, 64, axis=1)
    return jnp.where(lo, x, r), jnp.where(lo, r, x)


def _dot(a, b):
    return jnp.dot(a, b, preferred_element_type=F32)


def _dot_nt(a, b):
    return lax.dot_general(a, b, (((1,), (1,)), ((), ())), preferred_element_type=F32)


def _tree_sum(parts):
    while len(parts) > 1:
        parts = [parts[j] + parts[j + 1] if j + 1 < len(parts) else parts[j] for j in range(0, len(parts), 2)]
    return parts[0]


def _gelu_tanh(x):
    return 0.5 * x * (1.0 + jnp.tanh(0.7978845608028654 * (x + 0.044715 * (x * x * x))))


def _ones_halves(x):
    lo = _lane_iota(x.shape) < 64
    return jnp.where(lo, x, 1.0), jnp.where(lo, 1.0, x)


def _masked_softmax_pv(q_halves, k_of, v_of, bias_of, c_lo, c_hi, s_ref, rmax_ref, acc_ref, ck=CK, tail=None):
    n_e = len(q_halves)
    rows = q_halves[0].shape[0]
    r = rows // TQ
    for e in range(n_e):
        rmax_ref[e] = jnp.full((rows, LANES), NEG_INF, F32)
        acc_ref[e] = jnp.zeros((rows, LANES), F32)

    def pass1_at(off, w):
        k = k_of(off, w)
        bias = bias_of(off, w)
        for e in range(n_e):
            for j in range(r):
                rs = slice(j * TQ, (j + 1) * TQ)
                s = _dot_nt(q_halves[e][rs], k) + bias
                s_ref[e, rs, pl.ds(off, w)] = s
                cols = [s[:, i * LANES:(i + 1) * LANES] for i in range(w // LANES)]
                rmax_ref[e, rs] = functools.reduce(jnp.maximum, cols, rmax_ref[e, rs])

    def pass2_at(off, w):
        for e in range(n_e):
            v = v_of(e, off, w)
            for j in range(r):
                rs = slice(j * TQ, (j + 1) * TQ)
                mb = rmax_ref[e, rs]
                p = jnp.exp2(s_ref[e, rs, pl.ds(off, w)] - jnp.concatenate([mb] * (w // LANES), axis=1))
                acc_ref[e, rs] += _dot(p.astype(BF16), v)

    def sweep(body_at):
        def step(c, carry):
            body_at(pl.multiple_of(c * ck, ck), ck)
            return carry
        lax.fori_loop(c_lo, c_hi, step, 0)
        if tail is not None:
            has_tail, tail_w = tail

            @pl.when(has_tail)
            def _():
                body_at(pl.multiple_of(c_hi * ck, tail_w), tail_w)

    sweep(pass1_at)
    for e in range(n_e):
        m = jnp.max(rmax_ref[e], axis=-1, keepdims=True)
        m = jnp.where(m > NEG_INF, m, 0.0)
        rmax_ref[e] = jnp.broadcast_to(m, (rows, LANES))
    sweep(pass2_at)


def _norm_pair(acc_a, acc_b, lo):
    num = jnp.where(lo, acc_a, acc_b)
    den = pltpu.roll(jnp.where(lo, acc_b, acc_a), 64, axis=1)
    return num / jnp.where(den > 0.0, den, 1.0)


def _nsa_in_kernel(x_ref, g_ref, w_ref, cos_ref, sin_ref, q_ref, kv_ref, kc_ref, vc_ref, gate_ref):
    h = _rms(x_ref[...], g_ref[...]).astype(BF16)
    proj = _dot(h, w_ref[...])
    cos2 = cos_ref[...]
    sin2 = sin_ref[...]
    for m in range(8):
        slab = proj[:, m * LANES:(m + 1) * LANES]
        q_ref[:, m * LANES:(m + 1) * LANES] = (_rope128(slab, cos2, sin2) * Q_SCALE).astype(BF16)
    kc_ref[...] = proj[:, 1024:1152]
    vc_ref[...] = proj[:, 1152:1280]
    ks = _rope128(proj[:, 1280:1408], cos2, sin2)
    vs = proj[:, 1408:1536]
    kw = _rope128(proj[:, 1536:1664], cos2, sin2)
    vw = proj[:, 1664:1792]
    for j, (kk, vv) in enumerate(((ks, vs), (kw, vw))):
        k0, k1 = _dup_halves(kk)
        v0, v1 = _dup_halves(vv)
        for g, (kd, vd) in enumerate(((k0, v0), (k1, v1))):
            base = (g * 6 + j * 3) * LANES
            v_lo, v_hi = _ones_halves(vd)
            kv_ref[:, base:base + LANES] = kd.astype(BF16)
            kv_ref[:, base + LANES:base + 2 * LANES] = v_lo.astype(BF16)
            kv_ref[:, base + 2 * LANES:base + 3 * LANES] = v_hi.astype(BF16)
    gt = proj[:, 1792:1920]
    sg = 1.0 / (1.0 + jnp.exp(-gt))
    gate_ref[:, 0:LANES] = sg
    gate_ref[:, LANES:2 * LANES] = pltpu.roll(sg, LANES - 24, axis=1)


def _nsa_in_proj(x2d, g, w, cos2, sin2, S):
    T = x2d.shape[0]
    tm = IN_PROJ_TM
    tps = S // tm
    return pl.pallas_call(
        _nsa_in_kernel,
        grid=(T // tm,),
        in_specs=[
            pl.BlockSpec((tm, D_MODEL), lambda i: (i, 0)),
            pl.BlockSpec((1, D_MODEL), lambda i: (0, 0)),
            pl.BlockSpec((D_MODEL, IN_W_PAD), lambda i: (0, 0)),
            pl.BlockSpec((tm, LANES), lambda i: (i % tps, 0)),
            pl.BlockSpec((tm, LANES), lambda i: (i % tps, 0)),
        ],
        out_specs=[
            pl.BlockSpec((tm, 1024), lambda i: (i, 0)),
            pl.BlockSpec((tm, 12 * LANES), lambda i: (i, 0)),
            pl.BlockSpec((tm, LANES), lambda i: (i, 0)),
            pl.BlockSpec((tm, LANES), lambda i: (i, 0)),
            pl.BlockSpec((tm, 256), lambda i: (i, 0)),
        ],
        out_shape=[
            jax.ShapeDtypeStruct((T, 1024), BF16),
            jax.ShapeDtypeStruct((T, 12 * LANES), BF16),
            jax.ShapeDtypeStruct((T, LANES), F32),
            jax.ShapeDtypeStruct((T, LANES), F32),
            jax.ShapeDtypeStruct((T, 256), F32),
        ],
        compiler_params=_cparams(("parallel",)),
        name="nsa_in_proj",
    )(x2d, g, w, cos2, sin2)


def _nsa_cmp_kernel(kc_ref, vc_ref, pe_ref, w1_ref, w2_ref, cos_ref, sin_ref, o_ref):
    ncp = o_ref.shape[1]
    for kv, src in enumerate((kc_ref, vc_ref)):
        parts = []
        for l in range(CMP_STRIDE):
            xl = src[0, pl.ds(l, ncp, stride=CMP_STRIDE), :]
            xn = pltpu.roll(xl, ncp - 1, axis=0)
            parts.append(_dot((xl + pe_ref[kv, l:l + 1, :]).astype(BF16), w1_ref[kv, l]))
            ln = CMP_STRIDE + l
            parts.append(_dot((xn + pe_ref[kv, ln:ln + 1, :]).astype(BF16), w1_ref[kv, ln]))
        hid = _gelu_tanh(_tree_sum(parts))
        out = _dot(hid.astype(BF16), w2_ref[kv])
        for g in range(NSA_KV_GROUPS):
            slab = out[:, g * LANES:(g + 1) * LANES]
            if kv == 0:
                slab = _rope128(slab, cos_ref[...], sin_ref[...])
            j = kv * NSA_KV_GROUPS + g
            o_ref[0, :, j * LANES:(j + 1) * LANES] = slab.astype(BF16)


def _nsa_compress(kc, vc, pe2, w1g, w2g, cosk, sink):
    B, S, _ = kc.shape
    ncp = S // CMP_STRIDE
    full = lambda *shape: pl.BlockSpec(shape, lambda b: (0,) * len(shape))
    return pl.pallas_call(
        _nsa_cmp_kernel,
        grid=(B,),
        in_specs=[
            pl.BlockSpec((1, S, LANES), lambda b: (b, 0, 0)),
            pl.BlockSpec((1, S, LANES), lambda b: (b, 0, 0)),
            full(2, CMP_BLOCK, LANES),
            full(2, CMP_BLOCK, LANES, 2 * CMP_HIDDEN),
            full(2, 2 * CMP_HIDDEN, 2 * LANES),
            full(ncp, LANES),
            full(ncp, LANES),
        ],
        out_specs=pl.BlockSpec((1, ncp, 4 * LANES), lambda b: (b, 0, 0)),
        out_shape=jax.ShapeDtypeStruct((B, ncp, 4 * LANES), BF16),
        compiler_params=_cparams(("parallel",)),
        name="nsa_compress",
    )(kc, vc, pe2, w1g, w2g, cosk, sink)


def _nsa_attn_kernel(q_ref, kv_ref, cmp_ref, gate_ref, agg_ref, e_ref, o_ref,
                     s_ref, rmax_ref, acc_ref, *, S):
    n_sel = S // SEL_BLOCK
    top_n = min(SEL_TOP_N, n_sel)
    i = pl.program_id(2)
    s0 = i * TQ
    lo = _lane_iota((1, LANES)) < 64
    q = q_ref[0]
    q2 = jnp.concatenate([q[:, p * LANES:(p + 1) * LANES] for p in range(4)], axis=0)
    zero = jnp.zeros_like(q2)
    q_lo = jnp.where(lo, q2, zero)
    q_hi = jnp.where(lo, zero, q2)
    t = s0 + lax.broadcasted_iota(jnp.int32, (TQ, 1), 0)

    kc = cmp_ref[0, :, 0:LANES]
    ncp = kc.shape[0]
    cend = _lane_iota((1, ncp)) * CMP_STRIDE + (CMP_BLOCK - 1)
    mask_c = (cend <= t)[None]

    def cmp_branch(qh, vc1):
        s3 = jnp.where(mask_c, _dot_nt(qh, kc).reshape(4, TQ, ncp), NEG_INF)
        e = jnp.where(mask_c, jnp.exp2(s3 - jnp.max(s3, axis=-1, keepdims=True)), 0.0)
        e = e.reshape(4 * TQ, ncp)
        return e, _dot(e.astype(BF16), vc1)

    ea, acc_ca = cmp_branch(q_lo, cmp_ref[0, :, LANES:2 * LANES])
    eb, acc_cb = cmp_branch(q_hi, cmp_ref[0, :, 2 * LANES:3 * LANES])
    oc2 = _norm_pair(acc_ca, acc_cb, lo)
    den = pltpu.roll(jnp.where(lo, acc_cb, acc_ca), 64, axis=1)
    l_a = jnp.where(lo, den, acc_ca)
    l_b = jnp.where(lo, acc_cb, den)
    pa = ea / jnp.where(l_a > 0.0, l_a, 1.0)
    pb = eb / jnp.where(l_b > 0.0, l_b, 1.0)
    psum = _tree_sum([pa[p * TQ:(p + 1) * TQ] for p in range(4)] + [pb[p * TQ:(p + 1) * TQ] for p in range(4)])
    agg = agg_ref[...]
    p_hi = psum.astype(BF16)
    r1 = psum - p_hi.astype(F32)
    p_mid = r1.astype(BF16)
    p_lo = (r1 - p_mid.astype(F32)).astype(BF16)
    imp = _dot_nt(agg, p_hi) + _dot_nt(agg, p_mid) + _dot_nt(agg, p_lo)

    tq = s0 + _lane_iota((1, TQ))
    jblk = lax.broadcasted_iota(jnp.int32, (n_sel, 1), 0)
    cur = jnp.right_shift(tq, SEL_BLOCK.bit_length() - 1)
    forced = (jblk == 0) | (jblk == cur) | (jblk == cur - 1)
    causal_blk = jblk * SEL_BLOCK <= tq
    impm = jnp.where(causal_blk, jnp.where(forced, BIG, imp), NEG_INF)
    beats = []
    for k in range(n_sel):
        row = impm[k:k + 1, :]
        tie_k = jnp.where(jblk > k, 1.0, 0.0)
        beats.append(jnp.where(row > impm, 1.0, jnp.where(row == impm, tie_k, 0.0)))
    rank = _tree_sum(beats)
    sel_t = jnp.where(rank < float(top_n), 1.0, 0.0)
    sel = jnp.concatenate([sel_t, jnp.zeros((LANES - n_sel, TQ), F32)], axis=0).T.astype(BF16)
    q_halves = [q_lo, q_hi]
    c_hi = s0 // CK + 1

    def sel_bias(off, w):
        kpos = off + _lane_iota((1, w))
        picked = _dot(sel, e_ref[:, pl.ds(off, w)])
        return jnp.where(picked > 0.5, jnp.where(kpos <= t, 0.0, NEG_INF), NEG_INF)

    _masked_softmax_pv(q_halves,
                       lambda off, w: kv_ref[0, pl.ds(off, w), 0:LANES],
                       lambda e, off, w: kv_ref[0, pl.ds(off, w), (1 + e) * LANES:(2 + e) * LANES],
                       sel_bias, 0, c_hi // 2, s_ref, rmax_ref, acc_ref, ck=CK_WIDE, tail=(c_hi % 2 == 1, CK))
    os2 = _norm_pair(acc_ref[0], acc_ref[1], lo)

    def win_bias(off, w):
        dist = t - (off + _lane_iota((1, w)))
        return jnp.where(dist >= 0, jnp.where(dist < WINDOW, 0.0, NEG_INF), NEG_INF)

    c_lo = jnp.maximum(s0 - (WINDOW - 1), 0) // CK
    _masked_softmax_pv(q_halves,
                       lambda off, w: kv_ref[0, pl.ds(off, w), 3 * LANES:4 * LANES],
                       lambda e, off, w: kv_ref[0, pl.ds(off, w), (4 + e) * LANES:(5 + e) * LANES],
                       win_bias, c_lo, c_hi, s_ref, rmax_ref, acc_ref)
    ow2 = _norm_pair(acc_ref[0], acc_ref[1], lo)

    gt = gate_ref[0]
    for p in range(4):
        rs = slice(p * TQ, (p + 1) * TQ)
        ca, cb = (2 * p) * 3, (2 * p + 1) * 3
        g_c = jnp.where(lo, gt[:, ca:ca + 1], gt[:, cb:cb + 1])
        g_s = jnp.where(lo, gt[:, ca + 1:ca + 2], gt[:, cb + 1:cb + 2])
        g_w = jnp.where(lo, gt[:, ca + 2:ca + 3], gt[:, cb + 2:cb + 3])
        o_ref[0, :, p * LANES:(p + 1) * LANES] = (g_c * oc2[rs] + g_s * os2[rs] + g_w * ow2[rs]).astype(BF16)


def _nsa_attention(q, kv, cmpkv, gate, agg, emat, S):
    B = q.shape[0]
    ncp = cmpkv.shape[1]
    return pl.pallas_call(
        functools.partial(_nsa_attn_kernel, S=S),
        grid=(B, NSA_KV_GROUPS, S // TQ),
        in_specs=[
            pl.BlockSpec((1, TQ, 512), lambda b, g, i: (b, i, g)),
            pl.BlockSpec((1, S, 6 * LANES), lambda b, g, i: (b, 0, g)),
            pl.BlockSpec((1, ncp, 3 * LANES), lambda b, g, i: (b, 0, g)),
            pl.BlockSpec((1, TQ, LANES), lambda b, g, i: (b, i, g)),
            pl.BlockSpec((S // SEL_BLOCK, ncp), lambda b, g, i: (0, 0)),
            pl.BlockSpec((LANES, S), lambda b, g, i: (0, 0)),
        ],
        out_specs=pl.BlockSpec((1, TQ, 512), lambda b, g, i: (b, i, g)),
        out_shape=jax.ShapeDtypeStruct((B, S, 1024), BF16),
        scratch_shapes=[pltpu.VMEM((2, 4 * TQ, S), F32),
                        pltpu.VMEM((2, 4 * TQ, LANES), F32),
                        pltpu.VMEM((2, 4 * TQ, LANES), F32)],
        compiler_params=_cparams(("parallel", "parallel", "parallel")),
        name="nsa_attention",
    )(q, kv, cmpkv, gate, agg, emat)


def _dsa_in_kernel(x_ref, g_ref, w_ref, kvn_ref, wukv_ref, cos_ref, sin_ref,
                   q_ref, kv_ref, qi_ref, ki_ref, wi_ref):
    h = _rms(x_ref[...], g_ref[...]).astype(BF16)
    proj = _dot(h, w_ref[...])
    cos2 = cos_ref[...]
    sin2 = sin_ref[...]
    for m in range(8):
        slab = proj[:, m * LANES:(m + 1) * LANES]
        q_ref[:, m * LANES:(m + 1) * LANES] = (_rope128(slab, cos2, sin2) * Q_SCALE).astype(BF16)
    c = _rms(proj[:, 1024:1280], kvn_ref[...]).astype(BF16)
    kvu = _dot(c, wukv_ref[...])
    kd, vd = _dup_halves(kvu)
    v_lo, v_hi = _ones_halves(vd)
    kv_ref[:, 0:LANES] = _rope128(kd, cos2, sin2).astype(BF16)
    kv_ref[:, LANES:2 * LANES] = v_lo.astype(BF16)
    kv_ref[:, 2 * LANES:3 * LANES] = v_hi.astype(BF16)
    for m in range(4):
        slab = proj[:, 1280 + m * LANES:1280 + (m + 1) * LANES]
        qi_ref[:, m * LANES:(m + 1) * LANES] = (_rope128(slab, cos2, sin2) * (IDX_DIM ** -0.5)).astype(BF16)
    last = proj[:, 1792:1920]
    kid, wid = _dup_halves(last)
    ki_ref[...] = _rope128(kid, cos2, sin2).astype(BF16)
    wi_ref[0] = (wid * (IDX_HEADS ** -0.5)).T[0:IDX_HEADS, :]


def _dsa_in_proj(x2d, g, w, kvn, wukv, cos2, sin2, S):
    T = x2d.shape[0]
    tm = IN_PROJ_TM
    tps = S // tm
    return pl.pallas_call(
        _dsa_in_kernel,
        grid=(T // tm,),
        in_specs=[
            pl.BlockSpec((tm, D_MODEL), lambda i: (i, 0)),
            pl.BlockSpec((1, D_MODEL), lambda i: (0, 0)),
            pl.BlockSpec((D_MODEL, IN_W_PAD), lambda i: (0, 0)),
            pl.BlockSpec((1, DSA_KV_RANK), lambda i: (0, 0)),
            pl.BlockSpec((DSA_KV_RANK, LANES), lambda i: (0, 0)),
            pl.BlockSpec((tm, LANES), lambda i: (i % tps, 0)),
            pl.BlockSpec((tm, LANES), lambda i: (i % tps, 0)),
        ],
        out_specs=[
            pl.BlockSpec((tm, 1024), lambda i: (i, 0)),
            pl.BlockSpec((tm, 3 * LANES), lambda i: (i, 0)),
            pl.BlockSpec((tm, 512), lambda i: (i, 0)),
            pl.BlockSpec((tm, LANES), lambda i: (i, 0)),
            pl.BlockSpec((1, IDX_HEADS, tm), lambda i: (i // tps, 0, i % tps)),
        ],
        out_shape=[
            jax.ShapeDtypeStruct((T, 1024), BF16),
            jax.ShapeDtypeStruct((T, 3 * LANES), BF16),
            jax.ShapeDtypeStruct((T, 512), BF16),
            jax.ShapeDtypeStruct((T, LANES), BF16),
            jax.ShapeDtypeStruct((T // S, IDX_HEADS, S), F32),
        ],
        compiler_params=_cparams(("parallel",)),
        name="dsa_in_proj",
    )(x2d, g, w, kvn, wukv, cos2, sin2)


def _sortable(v):
    bits = lax.bitcast_convert_type(v + 0.0, jnp.int32)
    return bits ^ ((bits >> 31) & jnp.int32(0x7FFFFFFF))


_NEG_BITS = np.array([NEG_INF], np.float32).view(np.int32)
_NEG_KEY = int((_NEG_BITS ^ ((_NEG_BITS >> 31) & np.int32(0x7FFFFFFF)))[0])
_INT_MIN = -(2 ** 31)


def _dsa_attn_kernel(q_ref, kv_ref, qi_ref, ki_ref, wi_ref, o_ref,
                     key_ref, bias_ref, s_ref, rmax_ref, acc_ref, *, S):
    top_k = min(DSA_TOP_K_MAX, S // 4)
    idx_bits = (S - 1).bit_length()
    i = pl.program_id(1)
    s0 = i * TQ
    n_chunks = s0 // CK + 1
    lo = _lane_iota((1, LANES)) < 64
    tq = s0 + _lane_iota((1, TQ))

    qi = qi_ref[0]
    qi2 = jnp.concatenate([qi[:, p * LANES:(p + 1) * LANES] for p in range(4)], axis=0)
    zi = jnp.zeros_like(qi2)
    qi_lo = jnp.where(lo, qi2, zi)
    qi_hi = jnp.where(lo, zi, qi2)
    wt = wi_ref[0]

    def idx_body(c, carry):
        off = pl.multiple_of(c * CK, CK)
        ki = ki_ref[0, pl.ds(off, CK), :]
        sa = _dot_nt(ki, qi_lo)
        sb = _dot_nt(ki, qi_hi)
        score = jnp.zeros((CK, TQ), F32)
        for p in range(4):
            cs = slice(p * TQ, (p + 1) * TQ)
            score = score + jnp.maximum(sa[:, cs], 0.0) * wt[2 * p:2 * p + 1, :]
            score = score + jnp.maximum(sb[:, cs], 0.0) * wt[2 * p + 1:2 * p + 2, :]
        kpos = off + lax.broadcasted_iota(jnp.int32, (CK, 1), 0)
        score = jnp.where(kpos <= tq, score, NEG_INF)
        key_ref[pl.ds(off, CK), :] = _sortable(score)
        return carry

    lax.fori_loop(0, n_chunks, idx_body, 0)

    n_out = (S - n_chunks * CK).astype(F32)

    def count(pred_fn):
        def body(c, acc):
            off = pl.multiple_of(c * CK, CK)
            hit = jnp.where(pred_fn(key_ref[pl.ds(off, CK), :], off), 1.0, 0.0)
            return acc + _tree_sum([hit[8 * j:8 * j + 8] for j in range(CK // 8)])
        acc = lax.fori_loop(0, n_chunks, body, jnp.zeros((8, TQ), F32))
        return jnp.sum(acc, axis=0, keepdims=True)

    def bit_body(it, tu):
        bit = jnp.left_shift(jnp.int32(1), 31 - it)
        cand = (tu | bit) ^ jnp.int32(_INT_MIN)
        cnt = count(lambda kk, off: kk >= cand)
        cnt = cnt + n_out * jnp.where(jnp.int32(_NEG_KEY) >= cand, 1.0, 0.0)
        return jnp.where(cnt >= float(top_k), tu | bit, tu)

    tu = lax.fori_loop(0, 32, bit_body, jnp.zeros((1, TQ), jnp.int32))
    thr = tu ^ jnp.int32(_INT_MIN)
    n_gt = count(lambda kk, off: kk > thr)
    n_gt = n_gt + n_out * jnp.where(jnp.int32(_NEG_KEY) > thr, 1.0, 0.0)
    need = float(top_k) - n_gt
    n_eq = count(lambda kk, off: kk == thr)

    def row_iota(off):
        return off + lax.broadcasted_iota(jnp.int32, (CK, 1), 0)

    def tie_search():
        def tie_body(it, jb):
            bit = jnp.left_shift(jnp.int32(1), idx_bits - 1 - it)
            cand = jb | bit
            cnt = count(lambda kk, off: (kk == thr) & (row_iota(off) < cand))
            return jnp.where(cnt < need, cand, jb)
        return lax.fori_loop(0, idx_bits, tie_body, jnp.zeros((1, TQ), jnp.int32))

    jb = lax.cond(jnp.max(n_eq - need) > 0.0, tie_search,
                  lambda: jnp.full((1, TQ), 2 ** idx_bits - 1, jnp.int32))

    eye = jnp.where(lax.broadcasted_iota(jnp.int32, (TQ, TQ), 0) == _lane_iota((TQ, TQ)), 1.0, 0.0).astype(BF16)

    def bias_body(c, carry):
        off = pl.multiple_of(c * CK, CK)
        kk = key_ref[pl.ds(off, CK), :]
        kpos = row_iota(off)
        tie = jnp.where(kk == thr, jnp.where(kpos <= jb, 1.0, 0.0), 0.0)
        picked = jnp.where(kpos <= tq, jnp.where(kk > thr, 1.0, tie), 0.0)
        sel_qk = _dot_nt(eye, picked.astype(BF16))
        bias_ref[:, pl.ds(off, CK)] = (sel_qk - 1.0) * BIG
        return carry

    lax.fori_loop(0, n_chunks, bias_body, 0)

    q = q_ref[0]
    q2 = jnp.concatenate([q[:, p * LANES:(p + 1) * LANES] for p in range(8)], axis=0)
    zq = jnp.zeros_like(q2)
    for e, qh in enumerate((jnp.where(lo, q2, zq), jnp.where(lo, zq, q2))):
        _masked_softmax_pv([qh],
                           lambda off, w: kv_ref[0, pl.ds(off, w), 0:LANES],
                           lambda _, off, w, e=e: kv_ref[0, pl.ds(off, w), (1 + e) * LANES:(2 + e) * LANES],
                           lambda off, w: bias_ref[:, pl.ds(off, w)],
                           0, n_chunks // 2, s_ref, rmax_ref, acc_ref.at[pl.ds(e, 1)],
                           ck=CK_WIDE, tail=(n_chunks % 2 == 1, CK))
    o2 = _norm_pair(acc_ref[0], acc_ref[1], lo)
    for p in range(8):
        o_ref[0, :, p * LANES:(p + 1) * LANES] = o2[p * TQ:(p + 1) * TQ].astype(BF16)


def _dsa_attention(q, kv, qi, ki, wi, S):
    B = q.shape[0]
    return pl.pallas_call(
        functools.partial(_dsa_attn_kernel, S=S),
        grid=(B, S // TQ),
        in_specs=[
            pl.BlockSpec((1, TQ, 1024), lambda b, i: (b, i, 0)),
            pl.BlockSpec((1, S, 3 * LANES), lambda b, i: (b, 0, 0)),
            pl.BlockSpec((1, TQ, 512), lambda b, i: (b, i, 0)),
            pl.BlockSpec((1, S, LANES), lambda b, i: (b, 0, 0)),
            pl.BlockSpec((1, IDX_HEADS, TQ), lambda b, i: (b, 0, i)),
        ],
        out_specs=pl.BlockSpec((1, TQ, 1024), lambda b, i: (b, i, 0)),
        out_shape=jax.ShapeDtypeStruct((B, S, 1024), BF16),
        scratch_shapes=[pltpu.VMEM((S, TQ), jnp.int32),
                        pltpu.VMEM((TQ, S), F32),
                        pltpu.VMEM((1, 8 * TQ, S), F32),
                        pltpu.VMEM((1, 8 * TQ, LANES), F32),
                        pltpu.VMEM((2, 8 * TQ, LANES), F32)],
        compiler_params=_cparams(("parallel", "parallel")),
        name="dsa_attention",
    )(q, kv, qi, ki, wi)


def _out_proj_kernel(a_ref, w_ref, x_ref, g_ref, o_ref):
    y = _dot(a_ref[...], w_ref[...])
    o_ref[...] = x_ref[...] + _rms(y, g_ref[...])


def _out_proj(a2d, w, x2d, g):
    T = x2d.shape[0]
    tm = 512
    return pl.pallas_call(
        _out_proj_kernel,
        grid=(T // tm,),
        in_specs=[
            pl.BlockSpec((tm, 1024), lambda i: (i, 0)),
            pl.BlockSpec((1024, D_MODEL), lambda i: (0, 0)),
            pl.BlockSpec((tm, D_MODEL), lambda i: (i, 0)),
            pl.BlockSpec((1, D_MODEL), lambda i: (0, 0)),
        ],
        out_specs=pl.BlockSpec((tm, D_MODEL), lambda i: (i, 0)),
        out_shape=jax.ShapeDtypeStruct((T, D_MODEL), F32),
        compiler_params=_cparams(("parallel",)),
        name="mixer_out_proj",
    )(a2d, w, x2d, g)


FFN_TM = 512
FFN_FC = 256
FFN_NF = D_FF // FFN_FC
FFN_HALO = 16
FFN_DOWN_GROUP = 11
FFN_UBUFS = 4


def _ffn_kernel(x_ref, xh_ref, g2_ref, g3_ref, wup_ref, cw_ref, cb_ref, wd_ref, o_ref, hext, ubuf, act_ref,
                *, tiles_per_seq):
    i = pl.program_id(0)
    tm = FFN_TM
    g2 = g2_ref[...]
    x = x_ref[...]
    hext[FFN_HALO:, :] = _rms(x, g2).astype(BF16)
    hh = _rms(xh_ref[...], g2)
    hh = jnp.where(i % tiles_per_seq == 0, 0.0, hh)
    hext[0:FFN_HALO, :] = hh.astype(BF16)
    he = hext[...]

    def conv(u_ref, col):
        w = cw_ref[:, col:col + FFN_FC]
        return (w[2:3] * u_ref[FFN_HALO:FFN_HALO + tm, :]
                + w[1:2] * u_ref[FFN_HALO - 1:FFN_HALO - 1 + tm, :]
                + w[0:1] * u_ref[FFN_HALO - 2:FFN_HALO - 2 + tm, :]
                + cb_ref[:, col:col + FFN_FC])

    y = None
    for f in range(FFN_NF):
        cv = f * FFN_FC
        cg = D_FF + f * FFN_FC
        uv = ubuf.at[(2 * f) % FFN_UBUFS]
        ug = ubuf.at[(2 * f + 1) % FFN_UBUFS]
        uv[...] = _dot(he, wup_ref[:, cv:cv + FFN_FC])
        ug[...] = _dot(he, wup_ref[:, cg:cg + FFN_FC])
        act_ref[:, cv:cv + FFN_FC] = (_gelu_tanh(conv(ug, cg)) * conv(uv, cv)).astype(BF16)
        if (f + 1) % FFN_DOWN_GROUP == 0 or f == FFN_NF - 1:
            k0 = (f // FFN_DOWN_GROUP) * FFN_DOWN_GROUP * FFN_FC
            part = _dot(act_ref[:, k0:cv + FFN_FC], wd_ref[k0:cv + FFN_FC, :])
            y = part if y is None else y + part
    o_ref[...] = x + _rms(y, g3_ref[...])


def _ffn(x2d, g2, g3, wup, cw, cb, wd, S):
    T = x2d.shape[0]
    tm = FFN_TM
    hb = tm // FFN_HALO
    const = lambda i: (0, 0)
    return pl.pallas_call(
        functools.partial(_ffn_kernel, tiles_per_seq=S // tm),
        grid=(T // tm,),
        in_specs=[
            pl.BlockSpec((tm, D_MODEL), lambda i: (i, 0)),
            pl.BlockSpec((FFN_HALO, D_MODEL), lambda i: (jnp.maximum(i * hb - 1, 0), 0)),
            pl.BlockSpec((1, D_MODEL), const),
            pl.BlockSpec((1, D_MODEL), const),
            pl.BlockSpec((D_MODEL, 2 * D_FF), const),
            pl.BlockSpec((CONV_WIDTH, 2 * D_FF), const),
            pl.BlockSpec((1, 2 * D_FF), const),
            pl.BlockSpec((D_FF, D_MODEL), const),
        ],
        out_specs=pl.BlockSpec((tm, D_MODEL), lambda i: (i, 0)),
        out_shape=jax.ShapeDtypeStruct((T, D_MODEL), F32),
        scratch_shapes=[
            pltpu.VMEM((tm + FFN_HALO, D_MODEL), BF16),
            pltpu.VMEM((FFN_UBUFS, tm + FFN_HALO, FFN_FC), F32),
            pltpu.VMEM((tm, D_FF), BF16),
        ],
        compiler_params=_cparams(("parallel",)),
        name="conv_ffn",
    )(x2d, x2d, g2, g3, wup, cw, cb, wd)


def _rope_tables(pos):
    half = HEAD_DIM // 2
    inv = ROPE_THETA ** (-jnp.arange(half, dtype=F32) / half)
    ang = pos.astype(F32)[:, None] * inv[None, :]
    cos, sin = jnp.cos(ang), jnp.sin(ang)
    return jnp.tile(cos, (1, 4)), jnp.tile(jnp.concatenate([-sin, sin], axis=-1), (1, 2))


def _tables(S):
    ncp = S // CMP_STRIDE
    n_sel = S // SEL_BLOCK
    cos2, sin2 = _rope_tables(jnp.arange(S, dtype=jnp.int32))
    cosk, sink = _rope_tables(jnp.arange(ncp, dtype=jnp.int32) * CMP_STRIDE + (CMP_BLOCK - 1))
    cosc = jnp.stack([cosk, jnp.ones_like(cosk)])
    sinc = jnp.stack([sink, jnp.zeros_like(sink)])
    sj = np.arange(n_sel)[:, None]
    ci = np.arange(ncp)[None, :]
    agg_t = ((ci * CMP_STRIDE <= sj * SEL_BLOCK + SEL_BLOCK - 1)
             & (ci * CMP_STRIDE + CMP_BLOCK - 1 >= sj * SEL_BLOCK)
             & (ci < (S - CMP_BLOCK) // CMP_STRIDE + 1))
    agg_t = jnp.asarray(agg_t, dtype=BF16)
    emat = jnp.asarray(np.arange(LANES)[:, None] == (np.arange(S)[None, :] // SEL_BLOCK), dtype=BF16)
    return (cos2, sin2, cosc, sinc, agg_t, emat)


def _pad_cols(w, width):
    return jnp.pad(w, ((0, 0), (0, width - w.shape[1])))


def _nsa_mixer(x2d, g_pre, g_post, w_in, pe_k, w1_k, w2_k, pe_v, w1_v, w2_v, w_out, B, S, tabs):
    cos2, sin2, cosc, sinc, agg, emat = tabs
    ncp = S // CMP_STRIDE
    q, kv, kc, vc, gate = _nsa_in_proj(x2d, g_pre, _pad_cols(w_in, IN_W_PAD).astype(BF16), cos2, sin2, S)

    def block_diag2(w):
        z = jnp.zeros_like(w)
        return jnp.concatenate([jnp.concatenate([w, z], axis=-1), jnp.concatenate([z, w], axis=-1)], axis=-2)

    pe2 = jnp.stack([jnp.tile(pe_k, (1, 2)), jnp.tile(pe_v, (1, 2))])
    w1g = jnp.stack([block_diag2(w.reshape(CMP_BLOCK, HEAD_DIM, CMP_HIDDEN)) for w in (w1_k, w1_v)]).astype(BF16)
    w2g = jnp.stack([block_diag2(jnp.tile(w, (1, 2))) for w in (w2_k, w2_v)]).astype(BF16)
    cmp4 = _nsa_compress(kc.reshape(B, S, LANES), vc.reshape(B, S, LANES), pe2, w1g, w2g,
                         cosc[0], sinc[0])
    ones = jnp.ones((B, ncp, HEAD_DIM), BF16)
    cmpkv = jnp.concatenate(
        [piece for g in range(NSA_KV_GROUPS) for piece in (
            cmp4[:, :, g * LANES:(g + 1) * LANES],
            cmp4[:, :, (2 + g) * LANES:(2 + g) * LANES + HEAD_DIM], ones,
            ones, cmp4[:, :, (2 + g) * LANES:(2 + g) * LANES + HEAD_DIM])], axis=-1)
    o = _nsa_attention(q.reshape(B, S, 1024), kv.reshape(B, S, 12 * LANES), cmpkv,
                       gate.reshape(B, S, 256), agg, emat, S)
    return _out_proj(o.reshape(B * S, 1024), w_out.astype(BF16), x2d, g_post)


def _dsa_mixer(x2d, g_pre, g_post, w_in, kv_norm, w_uk, w_uv, w_out, B, S, tabs):
    cos2, sin2 = tabs[0], tabs[1]
    wukv = jnp.concatenate([w_uk, w_uv], axis=1).astype(BF16)
    q, kv, qi, ki, wi = _dsa_in_proj(x2d, g_pre, _pad_cols(w_in, IN_W_PAD).astype(BF16),
                                     kv_norm.reshape(1, -1), wukv, cos2, sin2, S)
    o = _dsa_attention(q.reshape(B, S, 1024), kv.reshape(B, S, 3 * LANES), qi.reshape(B, S, 512),
                       ki.reshape(B, S, LANES), wi, S)
    return _out_proj(o.reshape(B * S, 1024), w_out.astype(BF16), x2d, g_post)


def kernel(x, norm_g, nsa_w_in, nsa_cmp_pe_k, nsa_cmp_w1_k, nsa_cmp_w2_k, nsa_cmp_pe_v, nsa_cmp_w1_v, nsa_cmp_w2_v, nsa_w_out, dsa_w_in, dsa_kv_norm, dsa_w_uk, dsa_w_uv, dsa_w_out, ffn_w_up, ffn_conv_w, ffn_conv_b, ffn_w_down):
    B, S, _ = x.shape
    tabs = _tables(S)
    x2d = x.reshape(B * S, D_MODEL)
    for i in range(DEPTH):
        g = norm_g[i].reshape(4, 1, D_MODEL)
        j = i // 2
        if i % 2 == 0:
            x2d = _nsa_mixer(x2d, g[0], g[1], nsa_w_in[j], nsa_cmp_pe_k[j], nsa_cmp_w1_k[j], nsa_cmp_w2_k[j],
                             nsa_cmp_pe_v[j], nsa_cmp_w1_v[j], nsa_cmp_w2_v[j], nsa_w_out[j], B, S, tabs)
        else:
            x2d = _dsa_mixer(x2d, g[0], g[1], dsa_w_in[j], dsa_kv_norm[j], dsa_w_uk[j], dsa_w_uv[j],
                             dsa_w_out[j], B, S, tabs)
        x2d = _ffn(x2d, g[2], g[3], ffn_w_up[i].astype(BF16), ffn_conv_w[i],
                   ffn_conv_b[i].reshape(1, -1), ffn_w_down[i].astype(BF16), S)
    return x2d.reshape(B, S, D_MODEL)
```
